```python
import jax, jax.numpy as jnp
from jax import lax
import numpy as np

D_MODEL = 4096
BATCH = 1
SEQ = 16384
DEPTH = 1
DEC_BATCH = 8
DEC_SEQ = 16
PAST_LEN = 2048

CHUNK = 64
D_HG = D_MODEL // 2
HG_HEAD_DIM = 128
HG_HEADS = D_HG // HG_HEAD_DIM
D_RW = D_MODEL // 2
RW_HEAD_DIM = 64
RW_HEADS = D_RW // RW_HEAD_DIM
RW_DECAY_LORA = 128
RW_ICLR_LORA = 128
RW_GATE_LORA = 480
D_RW_IN = 3 * D_RW + RW_DECAY_LORA + RW_ICLR_LORA + RW_GATE_LORA
D_IN = 4 * D_HG + D_RW_IN + 2 * D_MODEL
D_FF = 11008
ALPHA = (2 * DEPTH) ** 0.25
BETA = (8 * DEPTH) ** -0.25
LN_EPS = 1e-5
RMS_EPS = 1e-6
RW_GN_EPS = 64e-5

kernel_name = "hgrn2_rwkv7_macaron_deepnorm_stream_step"


def _layer_norm(x, g, b):
    xf = x.astype(jnp.float32)
    mu = jnp.mean(xf, -1, keepdims=True)
    var = jnp.mean(jnp.square(xf - mu), -1, keepdims=True)
    return ((xf - mu) * lax.rsqrt(var + LN_EPS) * g + b).astype(x.dtype)


def _swiglu_ffn(x, w_in, w_down):
    gate, up = jnp.split(x @ w_in, 2, axis=-1)
    return (jax.nn.silu(gate) * up) @ w_down


def _hgrn2_chunked(q, k, v, logf, S0):
    B, T, H, DK = q.shape
    DV = v.shape[-1]
    C = CHUNK if T % CHUNK == 0 else T
    n = T // C
    to_blocks = lambda a: a.reshape(B, n, C, H, a.shape[-1]).transpose(1, 0, 3, 2, 4)
    causal = jnp.tril(jnp.ones((C, C), bool))[None, None, :, :, None]

    def step(S, inp):
        qi, ki, vi, gi = inp
        L = jnp.cumsum(gi, axis=2)
        diff = L[:, :, :, None, :] - L[:, :, None, :, :]
        decay = jnp.exp(jnp.where(causal, diff, -jnp.inf))
        scores = jnp.einsum('bhtd,bhsd,bhtsd->bhts', qi, ki, decay)
        o = jnp.einsum('bhts,bhsv->bhtv', scores, vi) + jnp.einsum('bhtd,bhdv->bhtv', qi * jnp.exp(L), S)
        L_end = L[:, :, -1:, :]
        S_new = jnp.exp(L_end[:, :, 0, :])[..., None] * S + jnp.einsum('bhsd,bhsv->bhdv', ki * jnp.exp(L_end - L), vi)
        return S_new, o

    S, o = lax.scan(step, S0, (to_blocks(q), to_blocks(k), to_blocks(v), to_blocks(logf)))
    return o.transpose(1, 0, 3, 2, 4).reshape(B, T, H, DV), S


def _rwkv7_scan(r, w, k, v, a, b, S0):
    def step(S, inp):
        rt, wt, kt, vt, at, bt = inp
        Sa = jnp.einsum('bhvk,bhk->bhv', S, at)
        S = S * wt[:, :, None, :] + Sa[..., None] * bt[:, :, None, :] + vt[..., None] * kt[:, :, None, :]
        return S, jnp.einsum('bhvk,bhk->bhv', S, rt)
    xs = tuple(t.transpose(1, 0, 2, 3) for t in (r, w, k, v, a, b))
    S, y = lax.scan(step, S0, xs)
    return y.transpose(1, 0, 2, 3), S


def _token_mix(h, hg_S0, rw_S0, shift0, lb, w_in, hg_norm_g, hg_proj, rw_mu, rw_w0, rw_w2, rw_a0, rw_a2,
               rw_g2, rw_k_k, rw_k_a, rw_r_k, rw_ln_g, rw_ln_b, rw_proj, w_out):
    f32 = jnp.float32
    B, T, _ = h.shape
    z = h @ w_in
    cuts = [D_HG, 2 * D_HG, 3 * D_HG, 4 * D_HG, 4 * D_HG + D_RW_IN, 4 * D_HG + D_RW_IN + D_MODEL]
    q, fpre, iv, gh, zr, ga, gb = jnp.split(z, cuts, axis=-1)

    fp = fpre.astype(f32)
    logf = jnp.log(lb + (1.0 - lb) * jax.nn.sigmoid(fp))
    kf = (1.0 - lb) * jax.nn.sigmoid(-fp)
    hs = lambda t: t.reshape(B, T, HG_HEADS, HG_HEAD_DIM)
    o, hg_S = _hgrn2_chunked(hs(q.astype(f32)), hs(kf), hs(iv.astype(f32)), hs(logf), hg_S0.astype(f32))
    o = o * lax.rsqrt(jnp.mean(o * o, -1, keepdims=True) + RMS_EPS)
    o = o.reshape(B, T, D_HG) * hg_norm_g * jax.nn.silu(gh.astype(f32))
    u_a = o.astype(h.dtype) @ hg_proj

    prev = jnp.concatenate([shift0.astype(zr.dtype), zr[:, :-1]], axis=1)
    zs = zr + (prev - zr) * rw_mu
    r, k, v, wd, ad, gd = jnp.split(
        zs, [D_RW, 2 * D_RW, 3 * D_RW, 3 * D_RW + RW_DECAY_LORA, 3 * D_RW + RW_DECAY_LORA + RW_ICLR_LORA], axis=-1)
    w_log = -jax.nn.softplus(-(rw_w0 + jnp.tanh(wd) @ rw_w2).astype(f32)) - 0.5
    decay = jnp.exp(-jnp.exp(w_log))
    a = jax.nn.sigmoid((rw_a0 + ad @ rw_a2).astype(f32))
    g = (jax.nn.sigmoid(gd) @ rw_g2).astype(f32)
    rs = lambda t: t.reshape(B, T, RW_HEADS, RW_HEAD_DIM)
    ph = lambda p: p.astype(f32).reshape(RW_HEADS, RW_HEAD_DIM)
    r_, k_, v_, a_, w_ = rs(r.astype(f32)), rs(k.astype(f32)), rs(v.astype(f32)), rs(a), rs(decay)
    kk = k_ * ph(rw_k_k)
    kk = kk / jnp.maximum(jnp.sqrt(jnp.sum(kk * kk, -1, keepdims=True)), 1e-12)
    k_ = k_ * (1.0 + (a_ - 1.0) * ph(rw_k_a))
    y, rw_S = _rwkv7_scan(r_, w_, k_, v_, -kk, kk * a_, rw_S0.astype(f32))
    mu = jnp.mean(y, -1, keepdims=True)
    var = jnp.mean(jnp.square(y - mu), -1, keepdims=True)
    y = ((y - mu) * lax.rsqrt(var + RW_GN_EPS)).reshape(B, T, D_RW) * rw_ln_g + rw_ln_b
    bonus = jnp.sum(r_ * k_ * ph(rw_r_k), -1, keepdims=True) * v_
    y = (y + bonus.reshape(B, T, D_RW)) * g
    u_b = y.astype(h.dtype) @ rw_proj

    m = jax.nn.sigmoid(ga) * u_a + jax.nn.sigmoid(gb) * u_b
    return m @ w_out, hg_S, rw_S, zr[:, -1:]


def setup_inputs(seed: int = 0) -> dict:
    key = jax.random.key(seed)
    ks = iter(jax.random.split(key, 64))
    nrm = lambda shape, s: jax.random.normal(next(ks), shape, jnp.float32) * s
    uni = lambda shape, lo, hi: jax.random.uniform(next(ks), shape, jnp.float32, lo, hi)
    gain = lambda shape: 1.0 + nrm(shape, 0.02)
    L = DEPTH
    return {
        "x_prompt": nrm((BATCH, SEQ, D_MODEL), 1.0),
        "x_sample": nrm((DEC_BATCH, DEC_SEQ, D_MODEL), 1.0),
        "state_hgrn": nrm((L, DEC_BATCH, HG_HEADS, HG_HEAD_DIM, HG_HEAD_DIM), 0.5),
        "state_rwkv": nrm((L, DEC_BATCH, RW_HEADS, RW_HEAD_DIM, RW_HEAD_DIM), 0.3),
        "state_shift": nrm((L, DEC_BATCH, 1, D_RW_IN), 1.0),
        "ln1_g": gain((L, D_MODEL)),
        "ln1_b": nrm((L, D_MODEL), 0.02),
        "ffn1_w_in": nrm((L, D_MODEL, 2 * D_FF), D_MODEL ** -0.5),
        "ffn1_w_down": nrm((L, D_FF, D_MODEL), BETA * D_FF ** -0.5),
        "ln2_g": gain((L, D_MODEL)),
        "ln2_b": nrm((L, D_MODEL), 0.02),
        "w_in": nrm((L, D_MODEL, D_IN), D_MODEL ** -0.5),
        "hg_lb": nrm((L + 1, D_HG), 0.3),
        "hg_norm_g": gain((L, D_HG)),
        "hg_proj": nrm((L, D_HG, D_MODEL), D_HG ** -0.5),
        "rw_mu": uni((L, D_RW_IN), 0.0, 1.0),
        "rw_w0": uni((L, D_RW), -6.0, -1.0),
        "rw_w2": nrm((L, RW_DECAY_LORA, D_RW), 0.5 * RW_DECAY_LORA ** -0.5),
        "rw_a0": nrm((L, D_RW), 0.1),
        "rw_a2": nrm((L, RW_ICLR_LORA, D_RW), RW_ICLR_LORA ** -0.5),
        "rw_g2": nrm((L, RW_GATE_LORA, D_RW), RW_GATE_LORA ** -0.5),
        "rw_k_k": 0.85 + nrm((L, D_RW), 0.02),
        "rw_k_a": gain((L, D_RW)),
        "rw_r_k": nrm((L, D_RW), 0.1),
        "rw_ln_g": gain((L, D_RW)),
        "rw_ln_b": nrm((L, D_RW), 0.02),
        "rw_proj": nrm((L, D_RW, D_MODEL), D_RW ** -0.5),
        "w_out": nrm((L, D_MODEL, D_MODEL), BETA * D_MODEL ** -0.5),
        "ln3_g": gain((L, D_MODEL)),
        "ln3_b": nrm((L, D_MODEL), 0.02),
        "ffn2_w_in": nrm((L, D_MODEL, 2 * D_FF), D_MODEL ** -0.5),
        "ffn2_w_down": nrm((L, D_FF, D_MODEL), BETA * D_FF ** -0.5),
    }


def reference(x_prompt, x_sample, state_hgrn, state_rwkv, state_shift, ln1_g, ln1_b, ffn1_w_in, ffn1_w_down,
              ln2_g, ln2_b, w_in, hg_lb, hg_norm_g, hg_proj, rw_mu, rw_w0, rw_w2, rw_a0, rw_a2, rw_g2, rw_k_k,
              rw_k_a, rw_r_k, rw_ln_g, rw_ln_b, rw_proj, w_out, ln3_g, ln3_b, ffn2_w_in, ffn2_w_down):
    f32 = jnp.float32
    lb_all = jnp.cumsum(jax.nn.softmax(hg_lb.astype(f32), axis=0), axis=0)

    def trunk(x, hg_S, rw_S, shift):
        new_hg, new_rw, new_sh = [], [], []
        for l in range(DEPTH):
            x = _layer_norm(ALPHA * x + 0.5 * _swiglu_ffn(x, ffn1_w_in[l], ffn1_w_down[l]), ln1_g[l], ln1_b[l])
            m, s_hg, s_rw, s_sh = _token_mix(
                x, hg_S[l], rw_S[l], shift[l], lb_all[l], w_in[l], hg_norm_g[l], hg_proj[l], rw_mu[l], rw_w0[l],
                rw_w2[l], rw_a0[l], rw_a2[l], rw_g2[l], rw_k_k[l], rw_k_a[l], rw_r_k[l], rw_ln_g[l], rw_ln_b[l],
                rw_proj[l], w_out[l])
            x = _layer_norm(ALPHA * x + m, ln2_g[l], ln2_b[l])
            x = _layer_norm(ALPHA * x + 0.5 * _swiglu_ffn(x, ffn2_w_in[l], ffn2_w_down[l]), ln3_g[l], ln3_b[l])
            new_hg.append(s_hg)
            new_rw.append(s_rw)
            new_sh.append(s_sh)
        return x, jnp.stack(new_hg), jnp.stack(new_rw), jnp.stack(new_sh)

    B = x_prompt.shape[0]
    hg0 = jnp.zeros((DEPTH, B, HG_HEADS, HG_HEAD_DIM, HG_HEAD_DIM), f32)
    rw0 = jnp.zeros((DEPTH, B, RW_HEADS, RW_HEAD_DIM, RW_HEAD_DIM), f32)
    sh0 = jnp.zeros((DEPTH, B, 1, D_RW_IN), x_prompt.dtype)
    y_prompt, hgrn_prompt, rwkv_prompt, shift_prompt = trunk(x_prompt, hg0, rw0, sh0)
    y_sample, hgrn_sample, rwkv_sample, shift_sample = trunk(x_sample, state_hgrn, state_rwkv, state_shift)
    return (y_prompt, y_sample, hgrn_prompt, rwkv_prompt, shift_prompt, hgrn_sample, rwkv_sample, shift_sample)
```

```python
import functools
import math

import jax
import jax.numpy as jnp
from jax import lax
from jax.experimental import pallas as pl
from jax.experimental.pallas import tpu as pltpu

F32 = jnp.float32
BF16 = jnp.bfloat16

LANES = 128
HG_HEAD_DIM = 128
RW_HEAD_DIM = 64
RW_PAIR = 2 * RW_HEAD_DIM
HG_BLOCK = 16
LN_EPS = 1e-5
RMS_EPS = 1e-6
RW_GN_EPS = 64e-5
VMEM_LIMIT = 56 * 1024 * 1024
PANEL_COLS = 512
PANEL_ROWS = 32

_NN = (((1,), (0,)), ((), ()))
_NT = (((1,), (1,)), ((), ()))
_TN = (((0,), (0,)), ((), ()))


def _mm(a, b, dims=_NN):
    return lax.dot_general(a.astype(BF16), b.astype(BF16), dims, preferred_element_type=F32)


def _mm_f32(a, b):
    return jnp.dot(a, b, precision=lax.Precision.HIGHEST, preferred_element_type=F32)


def _layer_norm(y, g, b):
    mu = jnp.mean(y, -1, keepdims=True)
    d = y - mu
    var = jnp.mean(d * d, -1, keepdims=True)
    return d * lax.rsqrt(var + LN_EPS) * g + b


def _params(*sem):
    return pltpu.CompilerParams(dimension_semantics=sem, vmem_limit_bytes=VMEM_LIMIT)


def _accumulate(o_ref, act, wd_ref, j, ncol):
    for n0 in range(0, o_ref.shape[1], ncol):
        part = jnp.dot(act, wd_ref[:, n0:n0 + ncol], preferred_element_type=F32)

        @pl.when(j == 0)
        def _():
            o_ref[:, n0:n0 + ncol] = part

        @pl.when(j > 0)
        def _():
            o_ref[:, n0:n0 + ncol] += part


def _residual_layer_norm(o_ref, x_ref, g_ref, b_ref, alpha, scale, nrow):
    g = g_ref[...]
    b = b_ref[...]

    def panel(i, carry):
        r0 = pl.multiple_of(i * nrow, nrow)
        y = alpha * x_ref[pl.ds(r0, nrow), :] + scale * o_ref[pl.ds(r0, nrow), :]
        o_ref[pl.ds(r0, nrow), :] = _layer_norm(y, g, b)
        return carry

    lax.fori_loop(0, o_ref.shape[0] // nrow, panel, 0)


def _ffn_body(x_ref, wg_ref, wu_ref, wd_ref, g_ref, b_ref, o_ref, xb_ref, *, alpha, nj):
    j = pl.program_id(1)

    @pl.when(j == 0)
    def _():
        xb_ref[...] = x_ref[...].astype(BF16)

    xb = xb_ref[...]
    h = jnp.dot(xb, wg_ref[...], preferred_element_type=F32)
    u = jnp.dot(xb, wu_ref[...], preferred_element_type=F32)
    act = (h * jax.nn.sigmoid(h) * u).astype(BF16)
    _accumulate(o_ref, act, wd_ref, j, PANEL_COLS)

    @pl.when(j == nj - 1)
    def _():
        _residual_layer_norm(o_ref, x_ref, g_ref, b_ref, alpha, 0.5, PANEL_ROWS)


def _ffn(x, w_in_b, w_down_b, ln_g, ln_b, *, alpha, tm, tf):
    M, D = x.shape
    F = w_down_b.shape[0]
    nj = F // tf
    return pl.pallas_call(
        functools.partial(_ffn_body, alpha=alpha, nj=nj),
        grid=(M // tm, nj),
        in_specs=[
            pl.BlockSpec((tm, D), lambda i, j: (i, 0)),
            pl.BlockSpec((D, tf), lambda i, j: (0, j)),
            pl.BlockSpec((D, tf), lambda i, j: (0, j + nj)),
            pl.BlockSpec((tf, D), lambda i, j: (j, 0)),
            pl.BlockSpec((1, D), lambda i, j: (0, 0)),
            pl.BlockSpec((1, D), lambda i, j: (0, 0)),
        ],
        out_specs=pl.BlockSpec((tm, D), lambda i, j: (i, 0)),
        out_shape=jax.ShapeDtypeStruct((M, D), F32),
        scratch_shapes=[pltpu.VMEM((tm, D), BF16)],
        compiler_params=_params("parallel", "arbitrary"),
        name="ffn",
    )(x, w_in_b, w_in_b, w_down_b, ln_g, ln_b)


def _zproj_body(x_ref, w_ref, o_ref, xb_ref):
    @pl.when(pl.program_id(1) == 0)
    def _():
        xb_ref[...] = x_ref[...].astype(BF16)

    o_ref[...] = jnp.dot(xb_ref[...], w_ref[...], preferred_element_type=F32)


def _zproj(x, w_b, *, tm, tn):
    M, D = x.shape
    N = w_b.shape[1]
    return pl.pallas_call(
        _zproj_body,
        grid=(M // tm, N // tn),
        in_specs=[pl.BlockSpec((tm, D), lambda i, j: (i, 0)), pl.BlockSpec((D, tn), lambda i, j: (0, j))],
        out_specs=pl.BlockSpec((tm, tn), lambda i, j: (i, j)),
        out_shape=jax.ShapeDtypeStruct((M, N), F32),
        scratch_shapes=[pltpu.VMEM((tm, D), BF16)],
        compiler_params=_params("parallel", "arbitrary"),
        name="zproj",
    )(x, w_b)


def _hgrn_body(q_ref, f_ref, v_ref, gh_ref, lb_ref, ng_ref, s0_ref, o_ref, so_ref, st_ref, *, nblk, nt):
    t = pl.program_id(2)
    c = HG_BLOCK

    @pl.when(t == 0)
    def _():
        st_ref[...] = s0_ref[...].T

    lb = lb_ref[...]
    ng = ng_ref[...]
    ri = lax.broadcasted_iota(jnp.int32, (c, c), 0)
    ci = lax.broadcasted_iota(jnp.int32, (c, c), 1)
    tri = (ci <= ri).astype(F32)
    rows = lax.broadcasted_iota(jnp.int32, (c, 1), 0)

    def block(i, st):
        r0 = pl.multiple_of(i * c, c)
        q = q_ref[pl.ds(r0, c), :]
        fp = f_ref[pl.ds(r0, c), :]
        v = v_ref[pl.ds(r0, c), :]
        gh = gh_ref[pl.ds(r0, c), :]
        logf = jnp.log(lb + (1.0 - lb) * jax.nn.sigmoid(fp))
        kf = (1.0 - lb) * jax.nn.sigmoid(-fp)
        L = _mm_f32(tri, logf)
        l_end = L[c - 1:c, :]
        o = _mm(q * jnp.exp(L), st, _NT)
        for s in range(c):
            dec = jnp.exp(jnp.where(rows >= s, L - L[s:s + 1, :], -jnp.inf))
            sc = jnp.sum(dec * q * kf[s:s + 1, :], -1, keepdims=True)
            o = o + sc * v[s:s + 1, :]
        st_new = st * jnp.exp(l_end) + _mm(v, kf * jnp.exp(l_end - L), _TN)
        o = o * lax.rsqrt(jnp.mean(o * o, -1, keepdims=True) + RMS_EPS)
        o = o * ng * (gh * jax.nn.sigmoid(gh))
        o_ref[pl.ds(r0, c), :] = o.astype(o_ref.dtype)
        return st_new

    st = lax.fori_loop(0, nblk, block, st_ref[...])
    st_ref[...] = st

    @pl.when(t == nt - 1)
    def _():
        so_ref[...] = st.T


def _hgrn(z, lb, norm_g, s0, *, B, T, tb, lay):
    nh = lay.d_hg // HG_HEAD_DIM
    nt = T // tb
    d = HG_HEAD_DIM
    zspec = lambda sec: pl.BlockSpec((tb, d), lambda b, h, t: (b * nt + t, sec * nh + h))
    pspec = pl.BlockSpec((1, d), lambda b, h, t: (0, h))
    sspec = pl.BlockSpec((None, None, d, d), lambda b, h, t: (b, h, 0, 0))
    return pl.pallas_call(
        functools.partial(_hgrn_body, nblk=tb // HG_BLOCK, nt=nt),
        grid=(B, nh, nt),
        in_specs=[zspec(0), zspec(1), zspec(2), zspec(3), pspec, pspec, sspec],
        out_specs=[pl.BlockSpec((tb, d), lambda b, h, t: (b * nt + t, h)), sspec],
        out_shape=[jax.ShapeDtypeStruct((B * T, lay.d_hg), BF16), jax.ShapeDtypeStruct(s0.shape, F32)],
        scratch_shapes=[pltpu.VMEM((d, d), F32)],
        compiler_params=_params("parallel", "parallel", "arbitrary"),
        name="hgrn",
    )(z, z, z, z, lb, norm_g, s0)


def _rwkv_chunk(r, lw, k, v, a, b, H):
    C, W = r.shape
    C2 = 2 * C
    lane = lax.broadcasted_iota(jnp.int32, (1, W), 1)
    in_a = lane < (W // 2)
    stack = lambda x: jnp.concatenate([jnp.where(in_a, x, 0.0), jnp.where(in_a, 0.0, x)], axis=0)
    ri = lax.broadcasted_iota(jnp.int32, (C, C), 0)
    ci = lax.broadcasted_iota(jnp.int32, (C, C), 1)
    Lc = _mm_f32((ci <= ri).astype(F32), lw)
    l_end = Lc[C - 1:C, :]
    e_inv = jnp.exp(-Lc)
    e_end = jnp.exp(l_end - Lc)
    As = stack(a * jnp.exp(Lc - lw))
    Rs = stack(r * jnp.exp(Lc))
    Bs = stack(b * e_inv)
    Ks = stack(k * e_inv)
    Bhs = stack(b * e_end)
    Khs = stack(k * e_end)
    Vs = stack(v)
    w_end = jnp.exp(l_end)

    row = lax.broadcasted_iota(jnp.int32, (C2, C2), 0)
    col = lax.broadcasted_iota(jnp.int32, (C2, C2), 1)
    if C2 % LANES == 0:
        sc = _mm(jnp.concatenate([As, Rs], axis=0), jnp.concatenate([Bs, Ks], axis=0), _NT)
        s_ab, s_ak, s_rb, s_rk = sc[:C2, :C2], sc[:C2, C2:], sc[C2:, :C2], sc[C2:, C2:]
    else:
        s_ab, s_ak, s_rb, s_rk = _mm(As, Bs, _NT), _mm(As, Ks, _NT), _mm(Rs, Bs, _NT), _mm(Rs, Ks, _NT)
    a_ab = jnp.where(col < row, s_ab, 0.0)
    a_ak = jnp.where(col < row, s_ak, 0.0)
    a_rb = jnp.where(col <= row, s_rb, 0.0)
    a_rk = jnp.where(col <= row, s_rk, 0.0)

    tinv = jnp.where(row == col, 1.0, 0.0) + a_ab
    pw = a_ab
    for _ in range(int(math.log2(C)) - 1):
        pw = _mm(pw, pw)
        tinv = tinv + _mm(tinv, pw)

    pq = _mm(tinv, jnp.concatenate([As, _mm(a_ak, Vs)], axis=1))
    P, Q = pq[:, :W], pq[:, W:]
    bp = _mm(Bhs, pq, _TN)
    rk = lax.broadcasted_iota(jnp.int32, (W, W), 0)
    ck = lax.broadcasted_iota(jnp.int32, (W, W), 1)
    M = jnp.where(rk == ck, jnp.broadcast_to(w_end, (W, W)), 0.0) + bp[:, :W]
    Z = bp[:, W:] + _mm(Khs, Vs, _TN)
    ry = Rs + _mm(a_rb, P)
    yc = _mm(a_rb, Q) + _mm(a_rk, Vs)
    ys = _mm(ry, H) + yc
    return ys[:C] + ys[C:], _mm(M, H) + Z


def _rwkv_body(r_ref, k_ref, v_ref, lo_ref, mur_ref, muk_ref, muv_ref, mulo_ref,
               sr_ref, sk_ref, sv_ref, slo_ref, w0_ref, w2_ref, a0_ref, a2_ref, g2_ref,
               kk_ref, ka_ref, rk_ref, lng_ref, lnb_ref, h0_ref,
               y_ref, ho_ref, h_sc, cr_sc, ck_sc, cv_sc, clo_sc, *, tb, chunk, nt, dl_w, dl_a, dl_g):
    t = pl.program_id(2)

    @pl.when(t == 0)
    def _():
        h_sc[...] = h0_ref[...]
        cr_sc[...] = sr_ref[...]
        ck_sc[...] = sk_ref[...]
        cv_sc[...] = sv_ref[...]
        clo_sc[...] = slo_ref[...]

    row = lax.broadcasted_iota(jnp.int32, (tb, 1), 0)

    def shifted(ref, carry, mu_ref):
        x = ref[...]
        prev = jnp.where(row == 0, carry[...], pltpu.roll(x, 1, 0))
        carry[...] = x[tb - 1:tb, :]
        return x + (prev - x) * mu_ref[...]

    rs = shifted(r_ref, cr_sc, mur_ref)
    ks = shifted(k_ref, ck_sc, muk_ref)
    vs = shifted(v_ref, cv_sc, muv_ref)
    lo = shifted(lo_ref, clo_sc, mulo_ref)
    wd = lo[:, :dl_w]
    ad = lo[:, dl_w:dl_w + dl_a]
    gd = lo[:, dl_w + dl_a:dl_w + dl_a + dl_g]

    li = lax.broadcasted_iota(jnp.int32, (RW_PAIR, RW_PAIR), 0) // RW_HEAD_DIM
    lj = lax.broadcasted_iota(jnp.int32, (RW_PAIR, RW_PAIR), 1) // RW_HEAD_DIM
    same_head = (li == lj).astype(F32)
    head_sum = lambda x: _mm_f32(x, same_head)

    w_log = -jax.nn.softplus(-(w0_ref[...] + _mm_f32(jnp.tanh(wd), w2_ref[...]))) - 0.5
    lw = -jnp.exp(w_log)
    a_lr = jax.nn.sigmoid(a0_ref[...] + _mm_f32(ad, a2_ref[...]))
    g = _mm(jax.nn.sigmoid(gd), g2_ref[...])
    kk = ks * kk_ref[...]
    kk = kk / jnp.maximum(jnp.sqrt(head_sum(kk * kk)), 1e-12)
    k2 = ks * (1.0 + (a_lr - 1.0) * ka_ref[...])
    a = -kk
    b = kk * a_lr

    H = h_sc[...]
    ys = []
    for c0 in range(0, tb, chunk):
        sl = slice(c0, c0 + chunk)
        yc, H = _rwkv_chunk(rs[sl], lw[sl], k2[sl], vs[sl], a[sl], b[sl], H)
        ys.append(yc)
    h_sc[...] = H
    y = ys[0] if len(ys) == 1 else jnp.concatenate(ys, axis=0)

    inv_n = 1.0 / RW_HEAD_DIM
    mu = head_sum(y) * inv_n
    d = y - mu
    var = head_sum(d * d) * inv_n
    yn = d * lax.rsqrt(var + RW_GN_EPS) * lng_ref[...] + lnb_ref[...]
    bonus = head_sum(rs * k2 * rk_ref[...]) * vs
    y_ref[...] = ((yn + bonus) * g).astype(y_ref.dtype)

    @pl.when(t == nt - 1)
    def _():
        ho_ref[...] = H


def _rwkv(z, mu_z, shift_z, prm, h0, *, B, T, tb, chunk, lay):
    npair = lay.d_rw // RW_PAIR
    nt = T // tb
    w = RW_PAIR
    lw_ = lay.lora_w
    c_r, c_k, c_v = (lay.off_r // w, lay.off_r // w + npair, lay.off_r // w + 2 * npair)
    c_lo = lay.off_lora // lw_
    zs = lambda c0: pl.BlockSpec((tb, w), lambda b, p, t: (b * nt + t, c0 + p))
    ms = lambda c0: pl.BlockSpec((1, w), lambda b, p, t: (0, c0 + p))
    ss = lambda c0: pl.BlockSpec((None, 1, w), lambda b, p, t: (b, 0, c0 + p))
    ps = pl.BlockSpec((1, w), lambda b, p, t: (0, p))
    ws = lambda rows: pl.BlockSpec((rows, w), lambda b, p, t: (0, p))
    hs = pl.BlockSpec((None, None, w, w), lambda b, p, t: (b, p, 0, 0))
    dl_w, dl_a, dl_g = prm["w2"].shape[0], prm["a2"].shape[0], prm["g2"].shape[0]
    return pl.pallas_call(
        functools.partial(_rwkv_body, tb=tb, chunk=chunk, nt=nt, dl_w=dl_w, dl_a=dl_a, dl_g=dl_g),
        grid=(B, npair, nt),
        in_specs=[
            zs(c_r), zs(c_k), zs(c_v), pl.BlockSpec((tb, lw_), lambda b, p, t: (b * nt + t, c_lo)),
            ms(c_r), ms(c_k), ms(c_v), pl.BlockSpec((1, lw_), lambda b, p, t: (0, c_lo)),
            ss(c_r), ss(c_k), ss(c_v), pl.BlockSpec((None, 1, lw_), lambda b, p, t: (b, 0, c_lo)),
            ps, ws(dl_w), ps, ws(dl_a), ws(dl_g), ps, ps, ps, ps, ps, hs,
        ],
        out_specs=[pl.BlockSpec((tb, w), lambda b, p, t: (b * nt + t, p)), hs],
        out_shape=[jax.ShapeDtypeStruct((B * T, lay.d_rw), BF16), jax.ShapeDtypeStruct(h0.shape, F32)],
        scratch_shapes=[pltpu.VMEM((w, w), F32), pltpu.VMEM((1, w), F32), pltpu.VMEM((1, w), F32),
                        pltpu.VMEM((1, w), F32), pltpu.VMEM((1, lw_), F32)],
        compiler_params=_params("parallel", "parallel", "arbitrary"),
        name="rwkv",
    )(z, z, z, z, mu_z, mu_z, mu_z, mu_z, shift_z, shift_z, shift_z, shift_z,
      prm["w0"], prm["w2"], prm["a0"], prm["a2"], prm["g2"], prm["k_k"], prm["k_a"], prm["r_k"],
      prm["ln_g"], prm["ln_b"], h0)


def _merge_body(o_ref, y_ref, ga_ref, gb_ref, x_ref, wa_ref, wb_ref, wo_ref, g_ref, b_ref, out_ref, *, alpha, nj):
    j = pl.program_id(1)
    ua = jnp.dot(o_ref[...], wa_ref[...], preferred_element_type=F32)
    ub = jnp.dot(y_ref[...], wb_ref[...], preferred_element_type=F32)
    m = jax.nn.sigmoid(ga_ref[...]) * ua + jax.nn.sigmoid(gb_ref[...]) * ub
    _accumulate(out_ref, m.astype(BF16), wo_ref, j, PANEL_COLS)

    @pl.when(j == nj - 1)
    def _():
        _residual_layer_norm(out_ref, x_ref, g_ref, b_ref, alpha, 1.0, PANEL_ROWS)


def _merge(o, y, z, x, wa_b, wb_b, wo_b, ln_g, ln_b, *, alpha, tm, tf, lay):
    M, D = x.shape
    nj = D // tf
    c_ga, c_gb = lay.off_ga // tf, lay.off_gb // tf
    return pl.pallas_call(
        functools.partial(_merge_body, alpha=alpha, nj=nj),
        grid=(M // tm, nj),
        in_specs=[
            pl.BlockSpec((tm, lay.d_hg), lambda i, j: (i, 0)),
            pl.BlockSpec((tm, lay.d_rw), lambda i, j: (i, 0)),
            pl.BlockSpec((tm, tf), lambda i, j: (i, c_ga + j)),
            pl.BlockSpec((tm, tf), lambda i, j: (i, c_gb + j)),
            pl.BlockSpec((tm, D), lambda i, j: (i, 0)),
            pl.BlockSpec((lay.d_hg, tf), lambda i, j: (0, j)),
            pl.BlockSpec((lay.d_rw, tf), lambda i, j: (0, j)),
            pl.BlockSpec((tf, D), lambda i, j: (j, 0)),
            pl.BlockSpec((1, D), lambda i, j: (0, 0)),
            pl.BlockSpec((1, D), lambda i, j: (0, 0)),
        ],
        out_specs=pl.BlockSpec((tm, D), lambda i, j: (i, 0)),
        out_shape=jax.ShapeDtypeStruct((M, D), F32),
        compiler_params=_params("parallel", "arbitrary"),
        name="merge",
    )(o, y, z, z, x, wa_b, wb_b, wo_b, ln_g, ln_b)


class _ZLayout:
    def __init__(self, d_model, dl_w, dl_a, dl_g):
        self.d_hg = d_model // 2
        self.d_rw = d_model // 2
        self.d_model = d_model
        self.n_lora = dl_w + dl_a + dl_g
        self.off_r = 4 * self.d_hg
        self.off_ga = self.off_r + 3 * self.d_rw
        self.off_gb = self.off_ga + d_model
        self.off_lora = self.off_gb + d_model
        self.lora_w = 1024
        assert self.n_lora <= self.lora_w and self.off_lora % self.lora_w == 0
        self.width = self.off_lora + self.lora_w
        self.ref_lora = self.off_r + 3 * self.d_rw
        self.ref_ga = self.ref_lora + self.n_lora

    def from_ref_cols(self, a):
        pad = jnp.zeros(a.shape[:-1] + (self.lora_w - self.n_lora,), a.dtype)
        return jnp.concatenate(
            [a[..., :self.ref_lora], a[..., self.ref_ga:], a[..., self.ref_lora:self.ref_ga], pad], axis=-1)

    def rw_in_to_z(self, a):
        lead = a.shape[:-1]
        zeros = lambda n: jnp.zeros(lead + (n,), a.dtype)
        return jnp.concatenate(
            [zeros(self.off_r), a[..., :3 * self.d_rw], zeros(2 * self.d_model), a[..., 3 * self.d_rw:],
             zeros(self.lora_w - self.n_lora)], axis=-1)

    def z_to_rw_in(self, zrow):
        return jnp.concatenate(
            [zrow[..., self.off_r:self.off_r + 3 * self.d_rw],
             zrow[..., self.off_lora:self.off_lora + self.n_lora]], axis=-1)


def _row_tile(m, pref):
    t = min(pref, m)
    while m % t:
        t //= 2
    return t


def kernel(x_prompt, x_sample, state_hgrn, state_rwkv, state_shift, ln1_g, ln1_b, ffn1_w_in, ffn1_w_down,
           ln2_g, ln2_b, w_in, hg_lb, hg_norm_g, hg_proj, rw_mu, rw_w0, rw_w2, rw_a0, rw_a2, rw_g2, rw_k_k,
           rw_k_a, rw_r_k, rw_ln_g, rw_ln_b, rw_proj, w_out, ln3_g, ln3_b, ffn2_w_in, ffn2_w_down):
    depth = ffn1_w_in.shape[0]
    assert depth == 1, "single-layer stack"
    D = x_prompt.shape[-1]
    alpha = (2 * depth) ** 0.25
    dl_w, dl_a, dl_g = rw_w2.shape[1], rw_a2.shape[1], rw_g2.shape[1]
    assert dl_w % LANES == 0 and dl_a % LANES == 0
    dl_gp = -(-dl_g // LANES) * LANES
    lay = _ZLayout(D, dl_w, dl_a, dl_g)
    nh = lay.d_hg // HG_HEAD_DIM
    nrw = lay.d_rw // RW_HEAD_DIM
    npair = nrw // 2
    l = 0

    row = lambda p: p[l].reshape(1, -1).astype(F32)
    f1_in, f1_dn = ffn1_w_in[l].astype(BF16), ffn1_w_down[l].astype(BF16)
    f2_in, f2_dn = ffn2_w_in[l].astype(BF16), ffn2_w_down[l].astype(BF16)
    w_in_z = lay.from_ref_cols(w_in[l]).astype(BF16)
    wa_b, wb_b, wo_b = hg_proj[l].astype(BF16), rw_proj[l].astype(BF16), w_out[l].astype(BF16)
    lb = jnp.cumsum(jax.nn.softmax(hg_lb.astype(F32), axis=0), axis=0)[l].reshape(1, -1)
    mu_z = lay.rw_in_to_z(rw_mu[l].reshape(1, -1))
    prm = dict(w0=row(rw_w0), w2=rw_w2[l], a0=row(rw_a0), a2=rw_a2[l],
               g2=jnp.pad(rw_g2[l], ((0, dl_gp - dl_g), (0, 0))).astype(BF16),
               k_k=row(rw_k_k), k_a=row(rw_k_a), r_k=row(rw_r_k), ln_g=row(rw_ln_g), ln_b=row(rw_ln_b))

    def trunk(x, hg_s0, rw_s0, shift0):
        B, T, _ = x.shape
        M = B * T
        x0 = x.reshape(M, D)
        tm = _row_tile(M, 512)
        x1 = _ffn(x0, f1_in, f1_dn, row(ln1_g), row(ln1_b), alpha=alpha, tm=tm, tf=256)
        z = _zproj(x1, w_in_z, tm=tm, tn=1024)
        tb = _row_tile(T, 256)
        o, hg_s = _hgrn(z, lb, row(hg_norm_g), hg_s0, B=B, T=T, tb=tb, lay=lay)
        st = jnp.swapaxes(rw_s0, -1, -2).reshape(B, npair, 2, RW_HEAD_DIM, RW_HEAD_DIM)
        zero = jnp.zeros_like(st[:, :, 0])
        h0 = jnp.concatenate([jnp.concatenate([st[:, :, 0], zero], -1),
                              jnp.concatenate([zero, st[:, :, 1]], -1)], -2)
        y, h_out = _rwkv(z, mu_z, lay.rw_in_to_z(shift0), prm, h0,
                         B=B, T=T, tb=tb, chunk=min(64, tb), lay=lay)
        hd = RW_HEAD_DIM
        rw_s = jnp.stack([h_out[:, :, :hd, :hd], h_out[:, :, hd:, hd:]], axis=2)
        rw_s = jnp.swapaxes(rw_s.reshape(B, nrw, hd, hd), -1, -2)
        x2 = _merge(o, y, z, x1, wa_b, wb_b, wo_b, row(ln2_g), row(ln2_b), alpha=alpha, tm=tm, tf=256, lay=lay)
        x3 = _ffn(x2, f2_in, f2_dn, row(ln3_g), row(ln3_b), alpha=alpha, tm=tm, tf=256)
        shift = lay.z_to_rw_in(z.reshape(B, T, -1)[:, -1:, :])
        return x3.reshape(B, T, D), hg_s[None], rw_s[None], shift[None]

    Bp = x_prompt.shape[0]
    hg0 = jnp.zeros((Bp, nh, HG_HEAD_DIM, HG_HEAD_DIM), F32)
    rw0 = jnp.zeros((Bp, nrw, RW_HEAD_DIM, RW_HEAD_DIM), F32)
    sh0 = jnp.zeros((Bp, 1, rw_mu.shape[-1]), F32)
    y_p, hg_p, rw_p, sh_p = trunk(x_prompt, hg0, rw0, sh0)
    y_s, hg_s, rw_s, sh_s = trunk(x_sample, state_hgrn[l].astype(F32), state_rwkv[l].astype(F32),
                                  state_shift[l].astype(F32))
    return (y_p, y_s, hg_p, rw_p, sh_p, hg_s, rw_s, sh_s)
```

```python
import functools
import math

import jax
import jax.numpy as jnp
from jax import lax
from jax.experimental import pallas as pl
from jax.experimental.pallas import tpu as pltpu

F32 = jnp.float32
BF16 = jnp.bfloat16

LANES = 128
HG_HEAD_DIM = 128
RW_HEAD_DIM = 64
RW_PAIR = 2 * RW_HEAD_DIM
HG_BLOCK = 16
RW_CHUNK = 64
LN_EPS = 1e-5
RMS_EPS = 1e-6
RW_GN_EPS = 64e-5
VMEM_LIMIT = 56 * 1024 * 1024
PANEL_COLS = 512
PANEL_ROWS = 32
FF_ALIGN = 1024

_NN = (((1,), (0,)), ((), ()))
_NT = (((1,), (1,)), ((), ()))
_TN = (((0,), (0,)), ((), ()))


def _mm(a, b, dims=_NN):
    return lax.dot_general(a.astype(BF16), b.astype(BF16), dims, preferred_element_type=F32)


def _mm_f32(a, b):
    return jnp.dot(a, b, precision=lax.Precision.HIGHEST, preferred_element_type=F32)


def _split3(x):
    hi = x.astype(BF16)
    r1 = x - hi.astype(F32)
    mid = r1.astype(BF16)
    lo = (r1 - mid.astype(F32)).astype(BF16)
    return hi, mid, lo


def _mask_mm(mask, x):
    mb = mask.astype(BF16)
    return sum(jnp.dot(mb, p, preferred_element_type=F32) for p in _split3(x))


def _mm_mask(x, mask):
    mb = mask.astype(BF16)
    return sum(jnp.dot(p, mb, preferred_element_type=F32) for p in _split3(x))


def _layer_norm(y, g, b):
    mu = jnp.mean(y, -1, keepdims=True)
    d = y - mu
    var = jnp.mean(d * d, -1, keepdims=True)
    return d * lax.rsqrt(var + LN_EPS) * g + b


def _params(*sem):
    return pltpu.CompilerParams(dimension_semantics=sem, vmem_limit_bytes=VMEM_LIMIT)


def _gated_up_body(x_ref, wg_ref, wu_ref, o_ref):
    x = x_ref[...]
    h = jnp.dot(x, wg_ref[...], preferred_element_type=F32)
    u = jnp.dot(x, wu_ref[...], preferred_element_type=F32)
    o_ref[...] = (h * jax.nn.sigmoid(h) * u).astype(o_ref.dtype)


def _gated_up(xb, w_in_b, *, tm, tn):
    M, D = xb.shape
    F = w_in_b.shape[1] // 2
    nj = F // tn
    return pl.pallas_call(
        _gated_up_body,
        grid=(M // tm, nj),
        in_specs=[pl.BlockSpec((tm, D), lambda i, j: (i, 0)),
                  pl.BlockSpec((D, tn), lambda i, j: (0, j)),
                  pl.BlockSpec((D, tn), lambda i, j: (0, j + nj))],
        out_specs=pl.BlockSpec((tm, tn), lambda i, j: (i, j)),
        out_shape=jax.ShapeDtypeStruct((M, F), BF16),
        compiler_params=_params("parallel", "arbitrary"),
        name="gated_up",
    )(xb, w_in_b, w_in_b)


def _accumulate(o_ref, act, w_ref, k, ncol):
    for n0 in range(0, o_ref.shape[1], ncol):
        part = jnp.dot(act, w_ref[:, n0:n0 + ncol], preferred_element_type=F32)

        @pl.when(k == 0)
        def _():
            o_ref[:, n0:n0 + ncol] = part

        @pl.when(k > 0)
        def _():
            o_ref[:, n0:n0 + ncol] += part


def _down_ln_body(a_ref, w_ref, x_ref, g_ref, b_ref, o_ref, *maybe_ob_ref, alpha, scale, nk):
    k = pl.program_id(1)
    _accumulate(o_ref, a_ref[...], w_ref, k, min(PANEL_COLS, o_ref.shape[1]))

    @pl.when(k == nk - 1)
    def _():
        g = g_ref[...]
        b = b_ref[...]
        nrow = min(PANEL_ROWS, o_ref.shape[0])

        def panel(i, carry):
            r0 = pl.multiple_of(i * nrow, nrow)
            rows = pl.ds(r0, nrow)
            y = _layer_norm(alpha * x_ref[rows, :] + scale * o_ref[rows, :], g, b)
            o_ref[rows, :] = y
            for ob_ref in maybe_ob_ref:
                ob_ref[rows, :] = y.astype(ob_ref.dtype)
            return carry

        lax.fori_loop(0, o_ref.shape[0] // nrow, panel, 0)


def _down_ln(act, w_b, x, ln_g, ln_b, *, alpha, scale, tm, tk, with_bf16):
    M, D = x.shape
    K = act.shape[1]
    nk = K // tk
    once = dict(pipeline_mode=pl.Buffered(1))
    row_spec = pl.BlockSpec((tm, D), lambda i, k: (i, 0))
    out_shape = [jax.ShapeDtypeStruct((M, D), F32)] + ([jax.ShapeDtypeStruct((M, D), BF16)] if with_bf16 else [])
    return pl.pallas_call(
        functools.partial(_down_ln_body, alpha=alpha, scale=scale, nk=nk),
        grid=(M // tm, nk),
        in_specs=[pl.BlockSpec((tm, tk), lambda i, k: (i, k)),
                  pl.BlockSpec((tk, D), lambda i, k: (k, 0)),
                  pl.BlockSpec((tm, D), lambda i, k: (i, 0), **once),
                  pl.BlockSpec((1, D), lambda i, k: (0, 0)),
                  pl.BlockSpec((1, D), lambda i, k: (0, 0))],
        out_specs=[row_spec] * len(out_shape),
        out_shape=out_shape,
        compiler_params=_params("parallel", "arbitrary"),
        name="down_ln",
    )(act, w_b, x, ln_g, ln_b)


def _zproj_body(x_ref, w_ref, o_ref):
    o_ref[...] = jnp.dot(x_ref[...], w_ref[...], preferred_element_type=F32)


def _zproj(xb, w_b, *, tm, tn):
    M, D = xb.shape
    N = w_b.shape[1]
    return pl.pallas_call(
        _zproj_body,
        grid=(M // tm, N // tn),
        in_specs=[pl.BlockSpec((tm, D), lambda i, j: (i, 0)), pl.BlockSpec((D, tn), lambda i, j: (0, j))],
        out_specs=pl.BlockSpec((tm, tn), lambda i, j: (i, j)),
        out_shape=jax.ShapeDtypeStruct((M, N), F32),
        compiler_params=_params("parallel", "arbitrary"),
        name="zproj",
    )(xb, w_b)


def _block_masks(n, c):
    ri = lax.broadcasted_iota(jnp.int32, (n, n), 0)
    ci = lax.broadcasted_iota(jnp.int32, (n, n), 1)
    same = (ri // c) == (ci // c)
    return jnp.where(same, 1.0, 0.0), jnp.where(same & (ci <= ri), 1.0, 0.0)


def _hgrn_body(q_ref, f_ref, v_ref, gh_ref, lb_ref, ng_ref, s0_ref, o_ref, so_ref, st_ref, *, tb, nt):
    t = pl.program_id(2)
    c = HG_BLOCK
    nb = tb // c
    d = HG_HEAD_DIM

    @pl.when(t == 0)
    def _():
        st_ref[...] = s0_ref[...].T

    lb = lb_ref[...]
    q = q_ref[...]
    fp = f_ref[...]
    v = v_ref[...]
    logf = jnp.log(lb + (1.0 - lb) * jax.nn.sigmoid(fp))
    kf = (1.0 - lb) * jax.nn.sigmoid(-fp)
    same, prefix = _block_masks(tb, c)
    L = _mask_mm(prefix, logf)
    l_end = _mask_mm(same, logf)
    qe = q * jnp.exp(L)
    ke = kf * jnp.exp(l_end - L)
    d_end = jnp.exp(l_end)

    L3, q3, kf3, v3 = (x.reshape(nb, c, d) for x in (L, q, kf, v))
    rows = lax.broadcasted_iota(jnp.int32, (1, c, 1), 1)
    o3 = jnp.zeros((nb, c, d), F32)
    for s in range(c):
        dec = jnp.exp(jnp.where(rows >= s, L3 - L3[:, s:s + 1, :], -jnp.inf))
        sc = jnp.sum(dec * q3 * kf3[:, s:s + 1, :], -1, keepdims=True)
        o3 = o3 + sc * v3[:, s:s + 1, :]
    o = o3.reshape(tb, d)

    blk = lambda x, i: x[i * c:(i + 1) * c, :]
    kv = [_mm(blk(v, i), blk(ke, i), _TN) for i in range(nb)]
    st = st_ref[...]
    starts = []
    for i in range(nb):
        starts.append(st)
        st = st * d_end[i * c:i * c + 1, :] + kv[i]
    st_ref[...] = st
    o = o + jnp.concatenate([_mm(blk(qe, i), starts[i], _NT) for i in range(nb)], axis=0)

    o = o * lax.rsqrt(jnp.mean(o * o, -1, keepdims=True) + RMS_EPS)
    gh = gh_ref[...]
    o_ref[...] = (o * ng_ref[...] * (gh * jax.nn.sigmoid(gh))).astype(o_ref.dtype)

    @pl.when(t == nt - 1)
    def _():
        so_ref[...] = st.T


def _hgrn(z, lb, norm_g, s0, *, B, T, tb, lay):
    nh = lay.d_hg // HG_HEAD_DIM
    nt = T // tb
    d = HG_HEAD_DIM
    zspec = lambda sec: pl.BlockSpec((tb, d), lambda b, h, t: (b * nt + t, sec * nh + h))
    pspec = pl.BlockSpec((1, d), lambda b, h, t: (0, h))
    sspec = pl.BlockSpec((None, None, d, d), lambda b, h, t: (b, h, 0, 0))
    return pl.pallas_call(
        functools.partial(_hgrn_body, tb=tb, nt=nt),
        grid=(B, nh, nt),
        in_specs=[zspec(0), zspec(1), zspec(2), zspec(3), pspec, pspec, sspec],
        out_specs=[pl.BlockSpec((tb, d), lambda b, h, t: (b * nt + t, h)), sspec],
        out_shape=[jax.ShapeDtypeStruct((B * T, lay.d_hg), BF16), jax.ShapeDtypeStruct(s0.shape, F32)],
        scratch_shapes=[pltpu.VMEM((d, d), F32)],
        compiler_params=_params("parallel", "parallel", "arbitrary"),
        name="hgrn",
    )(z, z, z, z, lb, norm_g, s0)


def _rwkv_chunks(r, lw, k, v, a, b, H, C):
    tb, W = r.shape
    nc = tb // C
    C2 = 2 * C
    chunks = range(nc)
    same, prefix = _block_masks(tb, C)
    Lc = _mask_mm(prefix, lw)
    l_end = _mask_mm(same, lw)
    e_inv = jnp.exp(-Lc)
    e_end = jnp.exp(l_end - Lc)
    w_end = jnp.exp(l_end)
    in_a = lax.broadcasted_iota(jnp.int32, (1, W), 1) < (W // 2)

    def stacked(x):
        xa = jnp.where(in_a, x, 0.0)
        xb = x - xa
        return [jnp.concatenate([xa[c * C:(c + 1) * C], xb[c * C:(c + 1) * C]], axis=0) for c in chunks]

    Rs = stacked(r * jnp.exp(Lc))
    As = [x.astype(BF16) for x in stacked(a * jnp.exp(Lc - lw))]
    Bs = stacked(b * e_inv)
    Ks = stacked(k * e_inv)
    Bhs = [x.astype(BF16) for x in stacked(b * e_end)]
    Khs = [x.astype(BF16) for x in stacked(k * e_end)]
    Vs = [x.astype(BF16) for x in stacked(v)]

    row = lax.broadcasted_iota(jnp.int32, (C2, C2), 0)
    col = lax.broadcasted_iota(jnp.int32, (C2, C2), 1)
    strict = col < row
    incl = col <= row
    if C2 % LANES == 0:
        sc = [_mm(jnp.concatenate([As[c], Rs[c]], axis=0), jnp.concatenate([Bs[c], Ks[c]], axis=0), _NT)
              for c in chunks]
        s_ab = [x[:C2, :C2] for x in sc]
        s_ak = [x[:C2, C2:] for x in sc]
        s_rb = [x[C2:, :C2] for x in sc]
        s_rk = [x[C2:, C2:] for x in sc]
    else:
        s_ab = [_mm(As[c], Bs[c], _NT) for c in chunks]
        s_ak = [_mm(As[c], Ks[c], _NT) for c in chunks]
        s_rb = [_mm(Rs[c], Bs[c], _NT) for c in chunks]
        s_rk = [_mm(Rs[c], Ks[c], _NT) for c in chunks]
    a_ab = [jnp.where(strict, x, 0.0) for x in s_ab]
    a_ak = [jnp.where(strict, x, 0.0).astype(BF16) for x in s_ak]
    a_rb = [jnp.where(incl, x, 0.0).astype(BF16) for x in s_rb]
    a_rk = [jnp.where(incl, x, 0.0).astype(BF16) for x in s_rk]

    eye = jnp.where(row == col, 1.0, 0.0)
    tinv = [eye + x for x in a_ab]
    pw = a_ab
    for _ in range(int(math.log2(C)) - 1):
        pw = [_mm(x, x) for x in pw]
        tinv = [tinv[c] + _mm(tinv[c], pw[c]) for c in chunks]

    av = [_mm(a_ak[c], Vs[c]) for c in chunks]
    pq = [_mm(tinv[c], jnp.concatenate([As[c], av[c].astype(BF16)], axis=1)) for c in chunks]
    pq_b = [x.astype(BF16) for x in pq]
    bp = [_mm(Bhs[c], pq_b[c], _TN) for c in chunks]
    kv = [_mm(Khs[c], Vs[c], _TN) for c in chunks]
    ry = [Rs[c] + _mm(a_rb[c], pq_b[c][:, :W]) for c in chunks]
    yc = [_mm(a_rb[c], pq_b[c][:, W:]) + _mm(a_rk[c], Vs[c]) for c in chunks]

    rk = lax.broadcasted_iota(jnp.int32, (W, W), 0)
    ck = lax.broadcasted_iota(jnp.int32, (W, W), 1)
    diag = rk == ck
    starts = []
    for c in chunks:
        starts.append(H)
        M = jnp.where(diag, jnp.broadcast_to(w_end[c * C:c * C + 1, :], (W, W)), 0.0) + bp[c][:, :W]
        H = _mm(M, H) + (bp[c][:, W:] + kv[c])
    ys = [_mm(ry[c], starts[c]) + yc[c] for c in chunks]
    ys = [x[:C] + x[C:] for x in ys]
    return (ys[0] if nc == 1 else jnp.concatenate(ys, axis=0)), H


def _rwkv_body(r_ref, k_ref, v_ref, lo_ref, mur_ref, muk_ref, muv_ref, mulo_ref,
               sr_ref, sk_ref, sv_ref, slo_ref, w0_ref, w2_ref, a0_ref, a2_ref, g2_ref,
               kk_ref, ka_ref, rk_ref, lng_ref, lnb_ref, h0_ref,
               y_ref, ho_ref, h_sc, cr_sc, ck_sc, cv_sc, clo_sc, *, tb, chunk, nt, dl_w, dl_a, dl_g):
    t = pl.program_id(2)

    @pl.when(t == 0)
    def _():
        h_sc[...] = h0_ref[...]
        cr_sc[...] = sr_ref[...]
        ck_sc[...] = sk_ref[...]
        cv_sc[...] = sv_ref[...]
        clo_sc[...] = slo_ref[...]

    row = lax.broadcasted_iota(jnp.int32, (tb, 1), 0)

    def shifted(ref, carry, mu_ref):
        x = ref[...]
        prev = jnp.where(row == 0, carry[...], pltpu.roll(x, 1, 0))
        carry[...] = x[tb - 1:tb, :]
        return x + (prev - x) * mu_ref[...]

    rs = shifted(r_ref, cr_sc, mur_ref)
    ks = shifted(k_ref, ck_sc, muk_ref)
    vs = shifted(v_ref, cv_sc, muv_ref)
    lo = shifted(lo_ref, clo_sc, mulo_ref)
    wd = lo[:, :dl_w]
    ad = lo[:, dl_w:dl_w + dl_a]
    gd = lo[:, dl_w + dl_a:dl_w + dl_a + dl_g]

    li = lax.broadcasted_iota(jnp.int32, (RW_PAIR, RW_PAIR), 0) // RW_HEAD_DIM
    lj = lax.broadcasted_iota(jnp.int32, (RW_PAIR, RW_PAIR), 1) // RW_HEAD_DIM
    same_head = jnp.where(li == lj, 1.0, 0.0)
    head_sum = lambda x: _mm_mask(x, same_head)

    w_log = -jax.nn.softplus(-(w0_ref[...] + _mm_f32(jnp.tanh(wd), w2_ref[...]))) - 0.5
    lw = -jnp.exp(w_log)
    a_lr = jax.nn.sigmoid(a0_ref[...] + _mm_f32(ad, a2_ref[...]))
    g = _mm(jax.nn.sigmoid(gd), g2_ref[...])
    kk = ks * kk_ref[...]
    kk = kk / jnp.maximum(jnp.sqrt(head_sum(kk * kk)), 1e-12)
    k2 = ks * (1.0 + (a_lr - 1.0) * ka_ref[...])

    y, H = _rwkv_chunks(rs, lw, k2, vs, -kk, kk * a_lr, h_sc[...], chunk)
    h_sc[...] = H

    inv_n = 1.0 / RW_HEAD_DIM
    mu = head_sum(y) * inv_n
    dev = y - mu
    var = head_sum(dev * dev) * inv_n
    yn = dev * lax.rsqrt(var + RW_GN_EPS) * lng_ref[...] + lnb_ref[...]
    bonus = head_sum(rs * k2 * rk_ref[...]) * vs
    y_ref[...] = ((yn + bonus) * g).astype(y_ref.dtype)

    @pl.when(t == nt - 1)
    def _():
        ho_ref[...] = H


def _rwkv(z, mu_z, shift_z, prm, h0, *, B, T, tb, chunk, lay):
    npair = lay.d_rw // RW_PAIR
    nt = T // tb
    w = RW_PAIR
    lw_ = lay.lora_w
    c_r, c_k, c_v = (lay.off_r // w, lay.off_r // w + npair, lay.off_r // w + 2 * npair)
    c_lo = lay.off_lora // lw_
    zs = lambda c0: pl.BlockSpec((tb, w), lambda b, p, t: (b * nt + t, c0 + p))
    ms = lambda c0: pl.BlockSpec((1, w), lambda b, p, t: (0, c0 + p))
    ss = lambda c0: pl.BlockSpec((None, 1, w), lambda b, p, t: (b, 0, c0 + p))
    ps = pl.BlockSpec((1, w), lambda b, p, t: (0, p))
    ws = lambda rows: pl.BlockSpec((rows, w), lambda b, p, t: (0, p))
    hs = pl.BlockSpec((None, None, w, w), lambda b, p, t: (b, p, 0, 0))
    dl_w, dl_a, dl_g = prm["w2"].shape[0], prm["a2"].shape[0], prm["g2"].shape[0]
    return pl.pallas_call(
        functools.partial(_rwkv_body, tb=tb, chunk=chunk, nt=nt, dl_w=dl_w, dl_a=dl_a, dl_g=dl_g),
        grid=(B, npair, nt),
        in_specs=[
            zs(c_r), zs(c_k), zs(c_v), pl.BlockSpec((tb, lw_), lambda b, p, t: (b * nt + t, c_lo)),
            ms(c_r), ms(c_k), ms(c_v), pl.BlockSpec((1, lw_), lambda b, p, t: (0, c_lo)),
            ss(c_r), ss(c_k), ss(c_v), pl.BlockSpec((None, 1, lw_), lambda b, p, t: (b, 0, c_lo)),
            ps, ws(dl_w), ps, ws(dl_a), ws(dl_g), ps, ps, ps, ps, ps, hs,
        ],
        out_specs=[pl.BlockSpec((tb, w), lambda b, p, t: (b * nt + t, p)), hs],
        out_shape=[jax.ShapeDtypeStruct((B * T, lay.d_rw), BF16), jax.ShapeDtypeStruct(h0.shape, F32)],
        scratch_shapes=[pltpu.VMEM((w, w), F32), pltpu.VMEM((1, w), F32), pltpu.VMEM((1, w), F32),
                        pltpu.VMEM((1, w), F32), pltpu.VMEM((1, lw_), F32)],
        compiler_params=_params("parallel", "parallel", "arbitrary"),
        name="rwkv",
    )(z, z, z, z, mu_z, mu_z, mu_z, mu_z, shift_z, shift_z, shift_z, shift_z,
      prm["w0"], prm["w2"], prm["a0"], prm["a2"], prm["g2"], prm["k_k"], prm["k_a"], prm["r_k"],
      prm["ln_g"], prm["ln_b"], h0)


def _merge_up_body(o_ref, y_ref, ga_ref, gb_ref, wa_ref, wb_ref, m_ref):
    ua = jnp.dot(o_ref[...], wa_ref[...], preferred_element_type=F32)
    ub = jnp.dot(y_ref[...], wb_ref[...], preferred_element_type=F32)
    m = jax.nn.sigmoid(ga_ref[...]) * ua + jax.nn.sigmoid(gb_ref[...]) * ub
    m_ref[...] = m.astype(m_ref.dtype)


def _merge_up(o, y, z, wa_b, wb_b, *, tm, tn, lay):
    M = o.shape[0]
    D = lay.d_model
    c_ga, c_gb = lay.off_ga // tn, lay.off_gb // tn
    return pl.pallas_call(
        _merge_up_body,
        grid=(M // tm, D // tn),
        in_specs=[
            pl.BlockSpec((tm, lay.d_hg), lambda i, j: (i, 0)),
            pl.BlockSpec((tm, lay.d_rw), lambda i, j: (i, 0)),
            pl.BlockSpec((tm, tn), lambda i, j: (i, c_ga + j)),
            pl.BlockSpec((tm, tn), lambda i, j: (i, c_gb + j)),
            pl.BlockSpec((lay.d_hg, tn), lambda i, j: (0, j)),
            pl.BlockSpec((lay.d_rw, tn), lambda i, j: (0, j)),
        ],
        out_specs=pl.BlockSpec((tm, tn), lambda i, j: (i, j)),
        out_shape=jax.ShapeDtypeStruct((M, D), BF16),
        compiler_params=_params("parallel", "arbitrary"),
        name="merge_up",
    )(o, y, z, z, wa_b, wb_b)


class _ZLayout:
    def __init__(self, d_model, dl_w, dl_a, dl_g):
        self.d_hg = d_model // 2
        self.d_rw = d_model // 2
        self.d_model = d_model
        self.n_lora = dl_w + dl_a + dl_g
        self.off_r = 4 * self.d_hg
        self.off_ga = self.off_r + 3 * self.d_rw
        self.off_gb = self.off_ga + d_model
        self.off_lora = self.off_gb + d_model
        self.lora_w = 1024
        assert self.n_lora <= self.lora_w and self.off_lora % self.lora_w == 0
        self.width = self.off_lora + self.lora_w
        self.ref_lora = self.off_r + 3 * self.d_rw
        self.ref_ga = self.ref_lora + self.n_lora

    def from_ref_cols(self, a):
        pad = jnp.zeros(a.shape[:-1] + (self.lora_w - self.n_lora,), a.dtype)
        return jnp.concatenate(
            [a[..., :self.ref_lora], a[..., self.ref_ga:], a[..., self.ref_lora:self.ref_ga], pad], axis=-1)

    def rw_in_to_z(self, a):
        lead = a.shape[:-1]
        zeros = lambda n: jnp.zeros(lead + (n,), a.dtype)
        return jnp.concatenate(
            [zeros(self.off_r), a[..., :3 * self.d_rw], zeros(2 * self.d_model), a[..., 3 * self.d_rw:],
             zeros(self.lora_w - self.n_lora)], axis=-1)

    def z_to_rw_in(self, zrow):
        return jnp.concatenate(
            [zrow[..., self.off_r:self.off_r + 3 * self.d_rw],
             zrow[..., self.off_lora:self.off_lora + self.n_lora]], axis=-1)


def _row_tile(m, pref):
    t = min(pref, m)
    while m % t:
        t //= 2
    return t


def _ffn_weights(w_in, w_down):
    F = w_down.shape[0]
    pad = -F % FF_ALIGN
    wg = jnp.pad(w_in[:, :F], ((0, 0), (0, pad)))
    wu = jnp.pad(w_in[:, F:], ((0, 0), (0, pad)))
    return (jnp.concatenate([wg, wu], axis=1).astype(BF16), jnp.pad(w_down, ((0, pad), (0, 0))).astype(BF16))


def kernel(x_prompt, x_sample, state_hgrn, state_rwkv, state_shift, ln1_g, ln1_b, ffn1_w_in, ffn1_w_down,
           ln2_g, ln2_b, w_in, hg_lb, hg_norm_g, hg_proj, rw_mu, rw_w0, rw_w2, rw_a0, rw_a2, rw_g2, rw_k_k,
           rw_k_a, rw_r_k, rw_ln_g, rw_ln_b, rw_proj, w_out, ln3_g, ln3_b, ffn2_w_in, ffn2_w_down):
    depth = ffn1_w_in.shape[0]
    assert depth == 1, "single-layer stack"
    D = x_prompt.shape[-1]
    alpha = (2 * depth) ** 0.25
    dl_w, dl_a, dl_g = rw_w2.shape[1], rw_a2.shape[1], rw_g2.shape[1]
    assert dl_w % LANES == 0 and dl_a % LANES == 0
    dl_gp = -(-dl_g // LANES) * LANES
    lay = _ZLayout(D, dl_w, dl_a, dl_g)
    nh = lay.d_hg // HG_HEAD_DIM
    nrw = lay.d_rw // RW_HEAD_DIM
    npair = nrw // 2
    l = 0

    row = lambda p: p[l].reshape(1, -1).astype(F32)
    f1_in, f1_dn = _ffn_weights(ffn1_w_in[l], ffn1_w_down[l])
    f2_in, f2_dn = _ffn_weights(ffn2_w_in[l], ffn2_w_down[l])
    w_in_z = lay.from_ref_cols(w_in[l]).astype(BF16)
    wa_b, wb_b, wo_b = hg_proj[l].astype(BF16), rw_proj[l].astype(BF16), w_out[l].astype(BF16)
    lb = jnp.cumsum(jax.nn.softmax(hg_lb.astype(F32), axis=0), axis=0)[l].reshape(1, -1)
    mu_z = lay.rw_in_to_z(rw_mu[l].reshape(1, -1))
    prm = dict(w0=row(rw_w0), w2=rw_w2[l], a0=row(rw_a0), a2=rw_a2[l],
               g2=jnp.pad(rw_g2[l], ((0, dl_gp - dl_g), (0, 0))).astype(BF16),
               k_k=row(rw_k_k), k_a=row(rw_k_a), r_k=row(rw_r_k), ln_g=row(rw_ln_g), ln_b=row(rw_ln_b))

    def ffn(x, xb, w_in_b, w_dn_b, g, b, tm_up, tm_dn, with_bf16):
        act = _gated_up(xb, w_in_b, tm=tm_up, tn=512)
        return _down_ln(act, w_dn_b, x, g, b, alpha=alpha, scale=0.5, tm=tm_dn, tk=FF_ALIGN, with_bf16=with_bf16)

    def trunk(x, hg_s0, rw_s0, shift0):
        B, T, _ = x.shape
        M = B * T
        x0 = x.reshape(M, D)
        tm_up = _row_tile(M, 1024)
        tm_dn = _row_tile(M, 512)
        x1, x1b = ffn(x0, x0.astype(BF16), f1_in, f1_dn, row(ln1_g), row(ln1_b), tm_up, tm_dn, True)
        z = _zproj(x1b, w_in_z, tm=tm_up, tn=1024)
        tb = _row_tile(T, 256)
        o, hg_s = _hgrn(z, lb, row(hg_norm_g), hg_s0, B=B, T=T, tb=tb, lay=lay)
        st = jnp.swapaxes(rw_s0, -1, -2).reshape(B, npair, 2, RW_HEAD_DIM, RW_HEAD_DIM)
        zero = jnp.zeros_like(st[:, :, 0])
        h0 = jnp.concatenate([jnp.concatenate([st[:, :, 0], zero], -1),
                              jnp.concatenate([zero, st[:, :, 1]], -1)], -2)
        y, h_out = _rwkv(z, mu_z, lay.rw_in_to_z(shift0), prm, h0,
                         B=B, T=T, tb=tb, chunk=min(RW_CHUNK, tb), lay=lay)
        hd = RW_HEAD_DIM
        rw_s = jnp.stack([h_out[:, :, :hd, :hd], h_out[:, :, hd:, hd:]], axis=2)
        rw_s = jnp.swapaxes(rw_s.reshape(B, nrw, hd, hd), -1, -2)
        m = _merge_up(o, y, z, wa_b, wb_b, tm=tm_up, tn=512, lay=lay)
        x2, x2b = _down_ln(m, wo_b, x1, row(ln2_g), row(ln2_b), alpha=alpha, scale=1.0,
                           tm=tm_dn, tk=1024, with_bf16=True)
        (x3,) = ffn(x2, x2b, f2_in, f2_dn, row(ln3_g), row(ln3_b), tm_up, tm_dn, False)
        shift = lay.z_to_rw_in(z.reshape(B, T, -1)[:, -1:, :])
        return x3.reshape(B, T, D), hg_s[None], rw_s[None], shift[None]

    Bp = x_prompt.shape[0]
    hg0 = jnp.zeros((Bp, nh, HG_HEAD_DIM, HG_HEAD_DIM), F32)
    rw0 = jnp.zeros((Bp, nrw, RW_HEAD_DIM, RW_HEAD_DIM), F32)
    sh0 = jnp.zeros((Bp, 1, rw_mu.shape[-1]), F32)
    y_p, hg_p, rw_p, sh_p = trunk(x_prompt, hg0, rw0, sh0)
    y_s, hg_s, rw_s, sh_s = trunk(x_sample, state_hgrn[l].astype(F32), state_rwkv[l].astype(F32),
                                  state_shift[l].astype(F32))
    return (y_p, y_s, hg_p, rw_p, sh_p, hg_s, rw_s, sh_s)
```

```python
import functools
import math

import jax
import jax.numpy as jnp
from jax import lax
from jax.experimental import pallas as pl
from jax.experimental.pallas import tpu as pltpu

F32 = jnp.float32
BF16 = jnp.bfloat16

LANES = 128
HG_HEAD_DIM = 128
RW_HEAD_DIM = 64
RW_PAIR = 2 * RW_HEAD_DIM
HG_BLOCK = 16
RW_CHUNK = 64
RECURRENCE_ROWS = 512
LN_EPS = 1e-5
RMS_EPS = 1e-6
RW_GN_EPS = 64e-5
VMEM_LIMIT = 56 * 1024 * 1024
PANEL_COLS = 512
PANEL_ROWS = 32
FF_ALIGN = 1024

_NN = (((1,), (0,)), ((), ()))
_NT = (((1,), (1,)), ((), ()))
_TN = (((0,), (0,)), ((), ()))


def _mm(a, b, dims=_NN):
    return lax.dot_general(a.astype(BF16), b.astype(BF16), dims, preferred_element_type=F32)


def _split(x, n):
    parts = []
    for _ in range(n - 1):
        p = x.astype(BF16)
        parts.append(p)
        x = x - p.astype(F32)
    parts.append(x.astype(BF16))
    return parts


def _mm_wide(a, b):
    ah, am = _split(a, 2)
    bh, bm = _split(b, 2)
    dot = lambda p, q: jnp.dot(p, q, preferred_element_type=F32)
    return dot(ah, bh) + (dot(ah, bm) + dot(am, bh))


def _mask_mm(mask, x, terms=3):
    mb = mask.astype(BF16)
    return sum(jnp.dot(mb, p, preferred_element_type=F32) for p in _split(x, terms))


def _mm_mask(x, mask, terms=3):
    mb = mask.astype(BF16)
    return sum(jnp.dot(p, mb, preferred_element_type=F32) for p in _split(x, terms))


def _layer_norm(y, g, b):
    mu = jnp.mean(y, -1, keepdims=True)
    d = y - mu
    var = jnp.mean(d * d, -1, keepdims=True)
    return d * lax.rsqrt(var + LN_EPS) * g + b


def _params(*sem):
    return pltpu.CompilerParams(dimension_semantics=sem, vmem_limit_bytes=VMEM_LIMIT)


def _gated_up_body(x_ref, wg_ref, wu_ref, o_ref):
    x = x_ref[...]
    h = jnp.dot(x, wg_ref[...], preferred_element_type=F32)
    u = jnp.dot(x, wu_ref[...], preferred_element_type=F32)
    o_ref[...] = (h * jax.nn.sigmoid(h) * u).astype(o_ref.dtype)


def _gated_up(xb, w_in_b, *, tm, tn):
    M, D = xb.shape
    F = w_in_b.shape[1] // 2
    nj = F // tn
    return pl.pallas_call(
        _gated_up_body,
        grid=(M // tm, nj),
        in_specs=[pl.BlockSpec((tm, D), lambda i, j: (i, 0)),
                  pl.BlockSpec((D, tn), lambda i, j: (0, j)),
                  pl.BlockSpec((D, tn), lambda i, j: (0, j + nj))],
        out_specs=pl.BlockSpec((tm, tn), lambda i, j: (i, j)),
        out_shape=jax.ShapeDtypeStruct((M, F), BF16),
        compiler_params=_params("parallel", "arbitrary"),
        name="gated_up",
    )(xb, w_in_b, w_in_b)


def _accumulate(o_ref, act, w_ref, k, ncol):
    @pl.when(k == 0)
    def _():
        o_ref[...] = jnp.zeros_like(o_ref)

    for n0 in range(0, o_ref.shape[1], ncol):
        o_ref[:, n0:n0 + ncol] += jnp.dot(act, w_ref[:, n0:n0 + ncol], preferred_element_type=F32)


def _down_ln_body(a_ref, w_ref, x_ref, g_ref, b_ref, o_ref, *maybe_ob_ref, alpha, scale, nk):
    k = pl.program_id(1)
    _accumulate(o_ref, a_ref[...], w_ref, k, min(PANEL_COLS, o_ref.shape[1]))

    @pl.when(k == nk - 1)
    def _():
        g = g_ref[...]
        b = b_ref[...]
        nrow = min(PANEL_ROWS, o_ref.shape[0])

        def panel(i, carry):
            r0 = pl.multiple_of(i * nrow, nrow)
            rows = pl.ds(r0, nrow)
            y = _layer_norm(alpha * x_ref[rows, :] + scale * o_ref[rows, :], g, b)
            o_ref[rows, :] = y
            for ob_ref in maybe_ob_ref:
                ob_ref[rows, :] = y.astype(ob_ref.dtype)
            return carry

        lax.fori_loop(0, o_ref.shape[0] // nrow, panel, 0)


def _down_ln(act, w_b, x, ln_g, ln_b, *, alpha, scale, tm, tk, with_bf16):
    M, D = x.shape
    K = act.shape[1]
    nk = K // tk
    once = dict(pipeline_mode=pl.Buffered(1))
    row_spec = pl.BlockSpec((tm, D), lambda i, k: (i, 0))
    out_shape = [jax.ShapeDtypeStruct((M, D), F32)] + ([jax.ShapeDtypeStruct((M, D), BF16)] if with_bf16 else [])
    return pl.pallas_call(
        functools.partial(_down_ln_body, alpha=alpha, scale=scale, nk=nk),
        grid=(M // tm, nk),
        in_specs=[pl.BlockSpec((tm, tk), lambda i, k: (i, k)),
                  pl.BlockSpec((tk, D), lambda i, k: (k, 0)),
                  pl.BlockSpec((tm, D), lambda i, k: (i, 0), **once),
                  pl.BlockSpec((1, D), lambda i, k: (0, 0)),
                  pl.BlockSpec((1, D), lambda i, k: (0, 0))],
        out_specs=[row_spec] * len(out_shape),
        out_shape=out_shape,
        compiler_params=_params("parallel", "arbitrary"),
        name="down_ln",
    )(act, w_b, x, ln_g, ln_b)


def _zproj_body(x_ref, w_ref, o_ref):
    o_ref[...] = jnp.dot(x_ref[...], w_ref[...], preferred_element_type=F32)


def _zproj(xb, w_b, *, n_cols, tm, tn):
    M, D = xb.shape
    N = n_cols
    assert N % tn == 0 and N <= w_b.shape[1]
    return pl.pallas_call(
        _zproj_body,
        grid=(M // tm, N // tn),
        in_specs=[pl.BlockSpec((tm, D), lambda i, j: (i, 0)), pl.BlockSpec((D, tn), lambda i, j: (0, j))],
        out_specs=pl.BlockSpec((tm, tn), lambda i, j: (i, j)),
        out_shape=jax.ShapeDtypeStruct((M, N), F32),
        compiler_params=_params("parallel", "arbitrary"),
        name="zproj",
    )(xb, w_b)


def _block_cumsum(x, c):
    n = x.shape[0]
    ri = lax.broadcasted_iota(jnp.int32, (n, n), 0)
    ci = lax.broadcasted_iota(jnp.int32, (n, n), 1)
    prefix = ((ri // c) == (ci // c)) & (ci <= ri)
    return _mask_mm(jnp.where(prefix, 1.0, 0.0), x)


def _hgrn_body(q_ref, f_ref, v_ref, gh_ref, lb_ref, ng_ref, s0_ref, o_ref, so_ref, st_ref, *, tb, nseq, nt):
    t = pl.program_id(2)
    c = HG_BLOCK
    nb = tb // c
    d = HG_HEAD_DIM

    @pl.when(t == 0)
    def _():
        for s in range(nseq):
            st_ref[s] = s0_ref[s].T

    lb = lb_ref[...]
    q = q_ref[...]
    fp = f_ref[...]
    v = v_ref[...]
    logf = jnp.log(lb + (1.0 - lb) * jax.nn.sigmoid(fp))
    kf = (1.0 - lb) * jax.nn.sigmoid(-fp)
    L = _block_cumsum(logf, c)
    L3, q3, kf3, v3 = (x.reshape(nb, c, d) for x in (L, q, kf, v))
    l_end = L3[:, c - 1:c, :]
    qe = q * jnp.exp(L)
    ke = (kf3 * jnp.exp(l_end - L3)).reshape(tb, d)
    d_end = jnp.exp(l_end)

    rows = lax.broadcasted_iota(jnp.int32, (1, c, 1), 1)
    o3 = jnp.zeros((nb, c, d), F32)
    for s in range(c):
        dec = jnp.exp(jnp.where(rows >= s, L3 - L3[:, s:s + 1, :], -jnp.inf))
        sc = jnp.sum(dec * q3 * kf3[:, s:s + 1, :], -1, keepdims=True)
        o3 = o3 + sc * v3[:, s:s + 1, :]
    o = o3.reshape(tb, d)

    blk = lambda x, i: x[i * c:(i + 1) * c, :]
    kv = [_mm(blk(v, i), blk(ke, i), _TN) for i in range(nb)]
    per_seq = nb // nseq
    states = [st_ref[s] for s in range(nseq)]
    starts = []
    for i in range(nb):
        s = i // per_seq
        starts.append(states[s])
        states[s] = states[s] * d_end[i] + kv[i]
    for s in range(nseq):
        st_ref[s] = states[s]
    o = o + jnp.concatenate([_mm(blk(qe, i), starts[i], _NT) for i in range(nb)], axis=0)

    o = o * lax.rsqrt(jnp.mean(o * o, -1, keepdims=True) + RMS_EPS)
    gh = gh_ref[...]
    o_ref[...] = (o * ng_ref[...] * (gh * jax.nn.sigmoid(gh))).astype(o_ref.dtype)

    @pl.when(t == nt - 1)
    def _():
        for s in range(nseq):
            so_ref[s] = states[s].T


def _seq_tiling(B, T, rows):
    if T >= rows:
        tb = _row_tile(T, rows)
        return 1, tb, T // tb
    nseq = _row_tile(B, max(1, rows // T))
    return nseq, nseq * T, 1


def _hgrn(z, lb, norm_g, s0, *, B, T, lay):
    nh = lay.d_hg // HG_HEAD_DIM
    nseq, tb, nt = _seq_tiling(B, T, RECURRENCE_ROWS)
    d = HG_HEAD_DIM
    zspec = lambda sec: pl.BlockSpec((tb, d), lambda b, h, t: (b * nt + t, sec * nh + h))
    pspec = pl.BlockSpec((1, d), lambda b, h, t: (0, h))
    sspec = pl.BlockSpec((nseq, None, d, d), lambda b, h, t: (b, h, 0, 0))
    return pl.pallas_call(
        functools.partial(_hgrn_body, tb=tb, nseq=nseq, nt=nt),
        grid=(B // nseq, nh, nt),
        in_specs=[zspec(0), zspec(1), zspec(2), zspec(3), pspec, pspec, sspec],
        out_specs=[pl.BlockSpec((tb, d), lambda b, h, t: (b * nt + t, h)), sspec],
        out_shape=[jax.ShapeDtypeStruct((B * T, lay.d_hg), BF16), jax.ShapeDtypeStruct(s0.shape, F32)],
        scratch_shapes=[pltpu.VMEM((nseq, d, d), F32)],
        compiler_params=_params("parallel", "parallel", "arbitrary"),
        name="hgrn",
    )(z, z, z, z, lb, norm_g, s0)


def _rwkv_chunks(r, lw, k, v, a, b, states, C):
    tb, W = r.shape
    nc = tb // C
    C2 = 2 * C
    chunks = range(nc)
    Lc = _block_cumsum(lw, C)
    Lc3 = Lc.reshape(nc, C, W)
    l_end = Lc3[:, C - 1:C, :]
    e_inv = jnp.exp(-Lc)
    e_end = jnp.exp(l_end - Lc3).reshape(tb, W)
    w_end = jnp.exp(l_end)
    in_a = lax.broadcasted_iota(jnp.int32, (1, W), 1) < (W // 2)

    def stacked(x):
        xa = jnp.where(in_a, x, 0.0)
        xb = x - xa
        return [jnp.concatenate([xa[c * C:(c + 1) * C], xb[c * C:(c + 1) * C]], axis=0) for c in chunks]

    Rs = stacked(r * jnp.exp(Lc))
    As = [x.astype(BF16) for x in stacked(a * jnp.exp(Lc - lw))]
    Bs = stacked(b * e_inv)
    Ks = stacked(k * e_inv)
    Bhs = [x.astype(BF16) for x in stacked(b * e_end)]
    Khs = [x.astype(BF16) for x in stacked(k * e_end)]
    Vs = [x.astype(BF16) for x in stacked(v)]

    row = lax.broadcasted_iota(jnp.int32, (C2, C2), 0)
    col = lax.broadcasted_iota(jnp.int32, (C2, C2), 1)
    strict = col < row
    incl = col <= row
    if C2 % LANES == 0:
        sc = [_mm(jnp.concatenate([As[c], Rs[c]], axis=0), jnp.concatenate([Bs[c], Ks[c]], axis=0), _NT)
              for c in chunks]
        s_ab = [x[:C2, :C2] for x in sc]
        s_ak = [x[:C2, C2:] for x in sc]
        s_rb = [x[C2:, :C2] for x in sc]
        s_rk = [x[C2:, C2:] for x in sc]
    else:
        s_ab = [_mm(As[c], Bs[c], _NT) for c in chunks]
        s_ak = [_mm(As[c], Ks[c], _NT) for c in chunks]
        s_rb = [_mm(Rs[c], Bs[c], _NT) for c in chunks]
        s_rk = [_mm(Rs[c], Ks[c], _NT) for c in chunks]
    a_ab = [jnp.where(strict, x, 0.0) for x in s_ab]
    a_ak = [jnp.where(strict, x, 0.0).astype(BF16) for x in s_ak]
    a_rb = [jnp.where(incl, x, 0.0).astype(BF16) for x in s_rb]
    a_rk = [jnp.where(incl, x, 0.0).astype(BF16) for x in s_rk]

    eye = jnp.where(row == col, 1.0, 0.0)
    tinv = [eye + x for x in a_ab]
    pw = a_ab
    for _ in range(int(math.log2(C)) - 1):
        pw = [_mm(x, x) for x in pw]
        tinv = [tinv[c] + _mm(tinv[c], pw[c]) for c in chunks]

    av = [_mm(a_ak[c], Vs[c]) for c in chunks]
    pq = [_mm(tinv[c], jnp.concatenate([As[c], av[c].astype(BF16)], axis=1)) for c in chunks]
    pq_b = [x.astype(BF16) for x in pq]
    bp = [_mm(Bhs[c], pq_b[c], _TN) for c in chunks]
    kv = [_mm(Khs[c], Vs[c], _TN) for c in chunks]
    ry = [Rs[c] + _mm(a_rb[c], pq_b[c][:, :W]) for c in chunks]
    yc = [_mm(a_rb[c], pq_b[c][:, W:]) + _mm(a_rk[c], Vs[c]) for c in chunks]

    rk = lax.broadcasted_iota(jnp.int32, (W, W), 0)
    ck = lax.broadcasted_iota(jnp.int32, (W, W), 1)
    diag = rk == ck
    states = list(states)
    per_seq = nc // len(states)
    starts = []
    for c in chunks:
        s = c // per_seq
        starts.append(states[s])
        M = jnp.where(diag, jnp.broadcast_to(w_end[c], (W, W)), 0.0) + bp[c][:, :W]
        states[s] = _mm(M, states[s]) + (bp[c][:, W:] + kv[c])
    ys = [_mm(ry[c], starts[c]) + yc[c] for c in chunks]
    ys = [x[:C] + x[C:] for x in ys]
    return (ys[0] if nc == 1 else jnp.concatenate(ys, axis=0)), states


def _rwkv_body(r_ref, k_ref, v_ref, lo_ref, mur_ref, muk_ref, muv_ref, mulo_ref,
               sr_ref, sk_ref, sv_ref, slo_ref, w0_ref, w2_ref, a0_ref, a2_ref, g2_ref,
               kk_ref, ka_ref, rk_ref, lng_ref, lnb_ref, h0_ref,
               y_ref, ho_ref, h_sc, cr_sc, ck_sc, cv_sc, clo_sc, *, tb, nseq, chunk, nt, dl_w, dl_a, dl_g):
    t = pl.program_id(2)
    tseq = tb // nseq

    @pl.when(t == 0)
    def _():
        h_sc[...] = h0_ref[...]
        cr_sc[...] = sr_ref[...]
        ck_sc[...] = sk_ref[...]
        cv_sc[...] = sv_ref[...]
        clo_sc[...] = slo_ref[...]

    seq_start = lax.broadcasted_iota(jnp.int32, (tb, 1), 0) % tseq == 0

    def shifted(ref, carry, mu_ref):
        x = ref[...]
        w = x.shape[1]
        before = jnp.broadcast_to(carry[...], (nseq, tseq, w)).reshape(tb, w)
        prev = jnp.where(seq_start, before, pltpu.roll(x, 1, 0))
        carry[...] = x.reshape(nseq, tseq, w)[:, tseq - 1:tseq, :]
        return x + (prev - x) * mu_ref[...]

    rs = shifted(r_ref, cr_sc, mur_ref)
    ks = shifted(k_ref, ck_sc, muk_ref)
    vs = shifted(v_ref, cv_sc, muv_ref)
    lo = shifted(lo_ref, clo_sc, mulo_ref)
    wd = lo[:, :dl_w]
    ad = lo[:, dl_w:dl_w + dl_a]
    gd = lo[:, dl_w + dl_a:dl_w + dl_a + dl_g]

    li = lax.broadcasted_iota(jnp.int32, (RW_PAIR, RW_PAIR), 0) // RW_HEAD_DIM
    lj = lax.broadcasted_iota(jnp.int32, (RW_PAIR, RW_PAIR), 1) // RW_HEAD_DIM
    same_head = jnp.where(li == lj, 1.0, 0.0)
    head_sum = lambda x: _mm_mask(x, same_head, terms=2)

    w_log = -jax.nn.softplus(-(w0_ref[...] + _mm_wide(jnp.tanh(wd), w2_ref[...]))) - 0.5
    lw = -jnp.exp(w_log)
    a_lr = jax.nn.sigmoid(a0_ref[...] + _mm_wide(ad, a2_ref[...]))
    g = _mm(jax.nn.sigmoid(gd), g2_ref[...])
    kk = ks * kk_ref[...]
    kk = kk / jnp.maximum(jnp.sqrt(head_sum(kk * kk)), 1e-12)
    k2 = ks * (1.0 + (a_lr - 1.0) * ka_ref[...])

    y, states = _rwkv_chunks(rs, lw, k2, vs, -kk, kk * a_lr, [h_sc[s] for s in range(nseq)], chunk)
    for s in range(nseq):
        h_sc[s] = states[s]

    inv_n = 1.0 / RW_HEAD_DIM
    mu = head_sum(y) * inv_n
    dev = y - mu
    var = head_sum(dev * dev) * inv_n
    yn = dev * lax.rsqrt(var + RW_GN_EPS) * lng_ref[...] + lnb_ref[...]
    bonus = head_sum(rs * k2 * rk_ref[...]) * vs
    y_ref[...] = ((yn + bonus) * g).astype(y_ref.dtype)

    @pl.when(t == nt - 1)
    def _():
        for s in range(nseq):
            ho_ref[s] = states[s]


def _rwkv(z, mu_z, shift_z, prm, h0, *, B, T, lay):
    npair = lay.d_rw // RW_PAIR
    nseq, tb, nt = _seq_tiling(B, T, RECURRENCE_ROWS)
    w = RW_PAIR
    lw_ = lay.lora_w
    c_r, c_k, c_v = (lay.off_r // w, lay.off_r // w + npair, lay.off_r // w + 2 * npair)
    c_lo = lay.off_lora // lw_
    zs = lambda c0: pl.BlockSpec((tb, w), lambda b, p, t: (b * nt + t, c0 + p))
    ms = lambda c0: pl.BlockSpec((1, w), lambda b, p, t: (0, c0 + p))
    ss = lambda c0: pl.BlockSpec((nseq, 1, w), lambda b, p, t: (b, 0, c0 + p))
    ps = pl.BlockSpec((1, w), lambda b, p, t: (0, p))
    ws = lambda rows: pl.BlockSpec((rows, w), lambda b, p, t: (0, p))
    hs = pl.BlockSpec((nseq, None, w, w), lambda b, p, t: (b, p, 0, 0))
    dl_w, dl_a, dl_g = prm["w2"].shape[0], prm["a2"].shape[0], prm["g2"].shape[0]
    return pl.pallas_call(
        functools.partial(_rwkv_body, tb=tb, nseq=nseq, chunk=min(RW_CHUNK, tb // nseq), nt=nt,
                          dl_w=dl_w, dl_a=dl_a, dl_g=dl_g),
        grid=(B // nseq, npair, nt),
        in_specs=[
            zs(c_r), zs(c_k), zs(c_v), pl.BlockSpec((tb, lw_), lambda b, p, t: (b * nt + t, c_lo)),
            ms(c_r), ms(c_k), ms(c_v), pl.BlockSpec((1, lw_), lambda b, p, t: (0, c_lo)),
            ss(c_r), ss(c_k), ss(c_v), pl.BlockSpec((nseq, 1, lw_), lambda b, p, t: (b, 0, c_lo)),
            ps, ws(dl_w), ps, ws(dl_a), ws(dl_g), ps, ps, ps, ps, ps, hs,
        ],
        out_specs=[pl.BlockSpec((tb, w), lambda b, p, t: (b * nt + t, p)), hs],
        out_shape=[jax.ShapeDtypeStruct((B * T, lay.d_rw), BF16), jax.ShapeDtypeStruct(h0.shape, F32)],
        scratch_shapes=[pltpu.VMEM((nseq, w, w), F32), pltpu.VMEM((nseq, 1, w), F32), pltpu.VMEM((nseq, 1, w), F32),
                        pltpu.VMEM((nseq, 1, w), F32), pltpu.VMEM((nseq, 1, lw_), F32)],
        compiler_params=_params("parallel", "parallel", "arbitrary"),
        name="rwkv",
    )(z, z, z, z, mu_z, mu_z, mu_z, mu_z, shift_z, shift_z, shift_z, shift_z,
      prm["w0"], prm["w2"], prm["a0"], prm["a2"], prm["g2"], prm["k_k"], prm["k_a"], prm["r_k"],
      prm["ln_g"], prm["ln_b"], h0)


def _merge_up_body(o_ref, y_ref, ga_ref, gb_ref, wa_ref, wb_ref, m_ref):
    ua = jnp.dot(o_ref[...], wa_ref[...], preferred_element_type=F32)
    ub = jnp.dot(y_ref[...], wb_ref[...], preferred_element_type=F32)
    m = jax.nn.sigmoid(ga_ref[...]) * ua + jax.nn.sigmoid(gb_ref[...]) * ub
    m_ref[...] = m.astype(m_ref.dtype)


def _merge_up(o, y, zg, wa_b, wb_b, *, tm, tn, lay):
    M = o.shape[0]
    D = lay.d_model
    c_ga, c_gb = 0, D // tn
    return pl.pallas_call(
        _merge_up_body,
        grid=(M // tm, D // tn),
        in_specs=[
            pl.BlockSpec((tm, lay.d_hg), lambda i, j: (i, 0)),
            pl.BlockSpec((tm, lay.d_rw), lambda i, j: (i, 0)),
            pl.BlockSpec((tm, tn), lambda i, j: (i, c_ga + j)),
            pl.BlockSpec((tm, tn), lambda i, j: (i, c_gb + j)),
            pl.BlockSpec((lay.d_hg, tn), lambda i, j: (0, j)),
            pl.BlockSpec((lay.d_rw, tn), lambda i, j: (0, j)),
        ],
        out_specs=pl.BlockSpec((tm, tn), lambda i, j: (i, j)),
        out_shape=jax.ShapeDtypeStruct((M, D), BF16),
        compiler_params=_params("parallel", "arbitrary"),
        name="merge_up",
    )(o, y, zg, zg, wa_b, wb_b)


class _ZLayout:
    def __init__(self, d_model, dl_w, dl_a, dl_g):
        self.d_hg = d_model // 2
        self.d_rw = d_model // 2
        self.d_model = d_model
        self.n_lora = dl_w + dl_a + dl_g
        self.off_r = 4 * self.d_hg
        self.off_lora = self.off_r + 3 * self.d_rw
        self.off_gates = self.off_lora + self.n_lora
        self.lora_w = 1024
        assert self.n_lora <= self.lora_w and self.off_lora % self.lora_w == 0
        self.width = self.off_lora + self.lora_w
        self.n_rw_in = 3 * self.d_rw + self.n_lora

    def rw_in_to_z(self, a):
        pads = [(0, 0)] * (a.ndim - 1) + [(self.off_r, self.width - self.off_r - self.n_rw_in)]
        return jnp.pad(a, pads)

    def z_to_rw_in(self, zrow):
        return zrow[..., self.off_r:self.off_r + self.n_rw_in]


def _row_tile(m, pref):
    t = min(pref, m)
    while m % t:
        t //= 2
    return t


def _ffn_weights(w_in, w_down):
    F = w_down.shape[0]
    pad = -F % FF_ALIGN
    wg = jnp.pad(w_in[:, :F], ((0, 0), (0, pad)))
    wu = jnp.pad(w_in[:, F:], ((0, 0), (0, pad)))
    return (jnp.concatenate([wg, wu], axis=1).astype(BF16), jnp.pad(w_down, ((0, pad), (0, 0))).astype(BF16))


def kernel(x_prompt, x_sample, state_hgrn, state_rwkv, state_shift, ln1_g, ln1_b, ffn1_w_in, ffn1_w_down,
           ln2_g, ln2_b, w_in, hg_lb, hg_norm_g, hg_proj, rw_mu, rw_w0, rw_w2, rw_a0, rw_a2, rw_g2, rw_k_k,
           rw_k_a, rw_r_k, rw_ln_g, rw_ln_b, rw_proj, w_out, ln3_g, ln3_b, ffn2_w_in, ffn2_w_down):
    depth = ffn1_w_in.shape[0]
    assert depth == 1, "single-layer stack"
    D = x_prompt.shape[-1]
    alpha = (2 * depth) ** 0.25
    dl_w, dl_a, dl_g = rw_w2.shape[1], rw_a2.shape[1], rw_g2.shape[1]
    assert dl_w % LANES == 0 and dl_a % LANES == 0
    dl_gp = -(-dl_g // LANES) * LANES
    lay = _ZLayout(D, dl_w, dl_a, dl_g)
    nh = lay.d_hg // HG_HEAD_DIM
    nrw = lay.d_rw // RW_HEAD_DIM
    npair = nrw // 2
    l = 0

    row = lambda p: p[l].reshape(1, -1).astype(F32)
    f1_in, f1_dn = _ffn_weights(ffn1_w_in[l], ffn1_w_down[l])
    f2_in, f2_dn = _ffn_weights(ffn2_w_in[l], ffn2_w_down[l])
    w_in_b = w_in[l].astype(BF16)
    w_gates_b = w_in_b[:, lay.off_gates:]
    assert w_gates_b.shape[1] == 2 * D
    wa_b, wb_b, wo_b = hg_proj[l].astype(BF16), rw_proj[l].astype(BF16), w_out[l].astype(BF16)
    lb = jnp.cumsum(jax.nn.softmax(hg_lb.astype(F32), axis=0), axis=0)[l].reshape(1, -1)
    mu_z = lay.rw_in_to_z(rw_mu[l].reshape(1, -1))
    prm = dict(w0=row(rw_w0), w2=rw_w2[l], a0=row(rw_a0), a2=rw_a2[l],
               g2=jnp.pad(rw_g2[l], ((0, dl_gp - dl_g), (0, 0))).astype(BF16),
               k_k=row(rw_k_k), k_a=row(rw_k_a), r_k=row(rw_r_k), ln_g=row(rw_ln_g), ln_b=row(rw_ln_b))

    def ffn(x, xb, w_in_b, w_dn_b, g, b, tm_up, tm_dn, with_bf16):
        act = _gated_up(xb, w_in_b, tm=tm_up, tn=512)
        return _down_ln(act, w_dn_b, x, g, b, alpha=alpha, scale=0.5, tm=tm_dn, tk=FF_ALIGN, with_bf16=with_bf16)

    def trunk(x, hg_s0, rw_s0, shift0):
        B, T, _ = x.shape
        M = B * T
        x0 = x.reshape(M, D)
        tm_up = _row_tile(M, 1024)
        tm_dn = _row_tile(M, 512)
        x1, x1b = ffn(x0, x0.astype(BF16), f1_in, f1_dn, row(ln1_g), row(ln1_b), tm_up, tm_dn, True)
        z = _zproj(x1b, w_in_b, n_cols=lay.width, tm=tm_up, tn=1024)
        zg = _zproj(x1b, w_gates_b, n_cols=2 * D, tm=tm_up, tn=1024)
        o, hg_s = _hgrn(z, lb, row(hg_norm_g), hg_s0, B=B, T=T, lay=lay)
        st = jnp.swapaxes(rw_s0, -1, -2).reshape(B, npair, 2, RW_HEAD_DIM, RW_HEAD_DIM)
        zero = jnp.zeros_like(st[:, :, 0])
        h0 = jnp.concatenate([jnp.concatenate([st[:, :, 0], zero], -1),
                              jnp.concatenate([zero, st[:, :, 1]], -1)], -2)
        y, h_out = _rwkv(z, mu_z, lay.rw_in_to_z(shift0), prm, h0, B=B, T=T, lay=lay)
        hd = RW_HEAD_DIM
        rw_s = jnp.stack([h_out[:, :, :hd, :hd], h_out[:, :, hd:, hd:]], axis=2)
        rw_s = jnp.swapaxes(rw_s.reshape(B, nrw, hd, hd), -1, -2)
        m = _merge_up(o, y, zg, wa_b, wb_b, tm=tm_up, tn=512, lay=lay)
        x2, x2b = _down_ln(m, wo_b, x1, row(ln2_g), row(ln2_b), alpha=alpha, scale=1.0,
                           tm=tm_dn, tk=1024, with_bf16=True)
        (x3,) = ffn(x2, x2b, f2_in, f2_dn, row(ln3_g), row(ln3_b), tm_up, tm_dn, False)
        shift = lay.z_to_rw_in(z.reshape(B, T, -1)[:, -1:, :])
        return x3.reshape(B, T, D), hg_s[None], rw_s[None], shift[None]

    Bp = x_prompt.shape[0]
    hg0 = jnp.zeros((Bp, nh, HG_HEAD_DIM, HG_HEAD_DIM), F32)
    rw0 = jnp.zeros((Bp, nrw, RW_HEAD_DIM, RW_HEAD_DIM), F32)
    sh0 = jnp.zeros((Bp, 1, rw_mu.shape[-1]), F32)
    y_p, hg_p, rw_p, sh_p = trunk(x_prompt, hg0, rw0, sh0)
    y_s, hg_s, rw_s, sh_s = trunk(x_sample, state_hgrn[l].astype(F32), state_rwkv[l].astype(F32),
                                  state_shift[l].astype(F32))
    return (y_p, y_s, hg_p, rw_p, sh_p, hg_s, rw_s, sh_s)
```

```python
import functools
import math

import jax
import jax.numpy as jnp
from jax import lax
from jax.experimental import pallas as pl
from jax.experimental.pallas import tpu as pltpu

F32 = jnp.float32
BF16 = jnp.bfloat16

LANES = 128
HG_HEAD_DIM = 128
RW_HEAD_DIM = 64
RW_PAIR = 2 * RW_HEAD_DIM
HG_BLOCK = 16
RW_CHUNK = 64
RECURRENCE_ROWS = 512
LN_EPS = 1e-5
RMS_EPS = 1e-6
RW_GN_EPS = 64e-5
VMEM_LIMIT = 56 * 1024 * 1024
PANEL_COLS = 512
PANEL_ROWS = 32
FF_ALIGN = 1024

_NN = (((1,), (0,)), ((), ()))
_NT = (((1,), (1,)), ((), ()))
_TN = (((0,), (0,)), ((), ()))


def _mm(a, b, dims=_NN):
    return lax.dot_general(a.astype(BF16), b.astype(BF16), dims, preferred_element_type=F32)


def _split(x, n):
    parts = []
    for _ in range(n - 1):
        p = x.astype(BF16)
        parts.append(p)
        x = x - p.astype(F32)
    parts.append(x.astype(BF16))
    return parts


def _mm_wide(a, b):
    ah, am = _split(a, 2)
    bh, bm = _split(b, 2)
    dot = lambda p, q: jnp.dot(p, q, preferred_element_type=F32)
    return dot(ah, bh) + (dot(ah, bm) + dot(am, bh))


def _mask_mm(mask, x, terms=3):
    mb = mask.astype(BF16)
    return sum(jnp.dot(mb, p, preferred_element_type=F32) for p in _split(x, terms))


def _mm_mask(x, mask, terms=3):
    mb = mask.astype(BF16)
    return sum(jnp.dot(p, mb, preferred_element_type=F32) for p in _split(x, terms))


def _layer_norm(y, g, b):
    mu = jnp.mean(y, -1, keepdims=True)
    d = y - mu
    var = jnp.mean(d * d, -1, keepdims=True)
    return d * lax.rsqrt(var + LN_EPS) * g + b


def _params(*sem):
    return pltpu.CompilerParams(dimension_semantics=sem, vmem_limit_bytes=VMEM_LIMIT)


def _gated_up_body(x_ref, wg_ref, wu_ref, o_ref):
    x = x_ref[...]
    h = jnp.dot(x, wg_ref[...], preferred_element_type=F32)
    u = jnp.dot(x, wu_ref[...], preferred_element_type=F32)
    o_ref[...] = (h * jax.nn.sigmoid(h) * u).astype(o_ref.dtype)


def _gated_up(xb, w_in_b, *, tm, tn):
    M, D = xb.shape
    F = w_in_b.shape[1] // 2
    nj = F // tn
    return pl.pallas_call(
        _gated_up_body,
        grid=(M // tm, nj),
        in_specs=[pl.BlockSpec((tm, D), lambda i, j: (i, 0)),
                  pl.BlockSpec((D, tn), lambda i, j: (0, j)),
                  pl.BlockSpec((D, tn), lambda i, j: (0, j + nj))],
        out_specs=pl.BlockSpec((tm, tn), lambda i, j: (i, j)),
        out_shape=jax.ShapeDtypeStruct((M, F), BF16),
        compiler_params=_params("parallel", "arbitrary"),
        name="gated_up",
    )(xb, w_in_b, w_in_b)


def _accumulate(o_ref, act, w_ref, k, ncol):
    @pl.when(k == 0)
    def _():
        o_ref[...] = jnp.zeros_like(o_ref)

    for n0 in range(0, o_ref.shape[1], ncol):
        o_ref[:, n0:n0 + ncol] += jnp.dot(act, w_ref[:, n0:n0 + ncol], preferred_element_type=F32)


def _down_ln_body(a_ref, w_ref, x_ref, g_ref, b_ref, o_ref, *maybe_ob_ref, alpha, scale, nk):
    k = pl.program_id(1)
    _accumulate(o_ref, a_ref[...], w_ref, k, min(PANEL_COLS, o_ref.shape[1]))

    @pl.when(k == nk - 1)
    def _():
        g = g_ref[...]
        b = b_ref[...]
        nrow = min(PANEL_ROWS, o_ref.shape[0])

        def panel(i, carry):
            r0 = pl.multiple_of(i * nrow, nrow)
            rows = pl.ds(r0, nrow)
            y = _layer_norm(alpha * x_ref[rows, :] + scale * o_ref[rows, :], g, b)
            o_ref[rows, :] = y
            for ob_ref in maybe_ob_ref:
                ob_ref[rows, :] = y.astype(ob_ref.dtype)
            return carry

        lax.fori_loop(0, o_ref.shape[0] // nrow, panel, 0)


def _down_ln(act, w_b, x, ln_g, ln_b, *, alpha, scale, tm, tk, with_bf16):
    M, D = x.shape
    K = act.shape[1]
    nk = K // tk
    once = dict(pipeline_mode=pl.Buffered(1))
    row_spec = pl.BlockSpec((tm, D), lambda i, k: (i, 0))
    out_shape = [jax.ShapeDtypeStruct((M, D), F32)] + ([jax.ShapeDtypeStruct((M, D), BF16)] if with_bf16 else [])
    return pl.pallas_call(
        functools.partial(_down_ln_body, alpha=alpha, scale=scale, nk=nk),
        grid=(M // tm, nk),
        in_specs=[pl.BlockSpec((tm, tk), lambda i, k: (i, k)),
                  pl.BlockSpec((tk, D), lambda i, k: (k, 0)),
                  pl.BlockSpec((tm, D), lambda i, k: (i, 0), **once),
                  pl.BlockSpec((1, D), lambda i, k: (0, 0)),
                  pl.BlockSpec((1, D), lambda i, k: (0, 0))],
        out_specs=[row_spec] * len(out_shape),
        out_shape=out_shape,
        compiler_params=_params("parallel", "arbitrary"),
        name="down_ln",
    )(act, w_b, x, ln_g, ln_b)


def _zproj_body(x_ref, w_ref, o_ref):
    o_ref[...] = jnp.dot(x_ref[...], w_ref[...], preferred_element_type=F32)


def _zproj(xb, w_b, *, n_cols, tm, tn):
    M, D = xb.shape
    N = n_cols
    assert N % tn == 0 and N <= w_b.shape[1]
    return pl.pallas_call(
        _zproj_body,
        grid=(M // tm, N // tn),
        in_specs=[pl.BlockSpec((tm, D), lambda i, j: (i, 0)), pl.BlockSpec((D, tn), lambda i, j: (0, j))],
        out_specs=pl.BlockSpec((tm, tn), lambda i, j: (i, j)),
        out_shape=jax.ShapeDtypeStruct((M, N), F32),
        compiler_params=_params("parallel", "arbitrary"),
        name="zproj",
    )(xb, w_b)


def _block_cumsum(x, c):
    n, w = x.shape
    nb = n // c
    ri = lax.broadcasted_iota(jnp.int32, (c, c), 0)
    ci = lax.broadcasted_iota(jnp.int32, (c, c), 1)
    wide = x if nb == 1 else jnp.concatenate([x[i * c:(i + 1) * c, :] for i in range(nb)], axis=1)
    out = _mask_mm(jnp.where(ci <= ri, 1.0, 0.0), wide)
    return out if nb == 1 else jnp.concatenate([out[:, i * w:(i + 1) * w] for i in range(nb)], axis=0)


def _hgrn_body(q_ref, f_ref, v_ref, gh_ref, lb_ref, ng_ref, s0_ref, o_ref, so_ref, st_ref, *, tb, nseq, nt):
    t = pl.program_id(2)
    c = HG_BLOCK
    nb = tb // c
    d = HG_HEAD_DIM

    @pl.when(t == 0)
    def _():
        for s in range(nseq):
            st_ref[s] = s0_ref[s].T

    lb = lb_ref[...]
    q = q_ref[...]
    fp = f_ref[...]
    v = v_ref[...]
    logf = jnp.log(lb + (1.0 - lb) * jax.nn.sigmoid(fp))
    kf = (1.0 - lb) * jax.nn.sigmoid(-fp)
    L = _block_cumsum(logf, c)
    L3, q3, kf3, v3 = (x.reshape(nb, c, d) for x in (L, q, kf, v))
    l_end = L3[:, c - 1:c, :]
    qe = q * jnp.exp(L)
    ke = (kf3 * jnp.exp(l_end - L3)).reshape(tb, d)
    d_end = jnp.exp(l_end)

    h = c // 2
    rows = lax.broadcasted_iota(jnp.int32, (1, h, 1), 1)
    halves = [(L3[:, :h, :], q3[:, :h, :]), (L3[:, h:, :], q3[:, h:, :])]
    acc = [jnp.zeros((nb, h, d), F32), jnp.zeros((nb, h, d), F32)]
    for s in range(c):
        l_s, k_s, v_s = L3[:, s:s + 1, :], kf3[:, s:s + 1, :], v3[:, s:s + 1, :]
        for half, (l_t, q_t) in enumerate(halves):
            first = s - half * h
            if first >= h:
                continue
            diff = l_t - l_s
            if first > 0:
                diff = jnp.where(rows >= first, diff, -jnp.inf)
            sc = jnp.sum(jnp.exp(diff) * q_t * k_s, -1, keepdims=True)
            acc[half] = acc[half] + sc * v_s
    o = jnp.concatenate(acc, axis=1).reshape(tb, d)

    blk = lambda x, i: x[i * c:(i + 1) * c, :]
    kv = [_mm(blk(v, i), blk(ke, i), _TN) for i in range(nb)]
    per_seq = nb // nseq
    states = [st_ref[s] for s in range(nseq)]
    starts = []
    for i in range(nb):
        s = i // per_seq
        starts.append(states[s])
        states[s] = states[s] * d_end[i] + kv[i]
    for s in range(nseq):
        st_ref[s] = states[s]
    o = o + jnp.concatenate([_mm(blk(qe, i), starts[i], _NT) for i in range(nb)], axis=0)

    o = o * lax.rsqrt(jnp.mean(o * o, -1, keepdims=True) + RMS_EPS)
    gh = gh_ref[...]
    o_ref[...] = (o * ng_ref[...] * (gh * jax.nn.sigmoid(gh))).astype(o_ref.dtype)

    @pl.when(t == nt - 1)
    def _():
        for s in range(nseq):
            so_ref[s] = states[s].T


def _seq_tiling(B, T, rows):
    if T >= rows:
        tb = _row_tile(T, rows)
        return 1, tb, T // tb
    nseq = _row_tile(B, max(1, rows // T))
    return nseq, nseq * T, 1


def _hgrn(z, lb, norm_g, s0, *, B, T, lay):
    nh = lay.d_hg // HG_HEAD_DIM
    nseq, tb, nt = _seq_tiling(B, T, RECURRENCE_ROWS)
    d = HG_HEAD_DIM
    zspec = lambda sec: pl.BlockSpec((tb, d), lambda b, h, t: (b * nt + t, sec * nh + h))
    pspec = pl.BlockSpec((1, d), lambda b, h, t: (0, h))
    sspec = pl.BlockSpec((nseq, None, d, d), lambda b, h, t: (b, h, 0, 0))
    return pl.pallas_call(
        functools.partial(_hgrn_body, tb=tb, nseq=nseq, nt=nt),
        grid=(B // nseq, nh, nt),
        in_specs=[zspec(0), zspec(1), zspec(2), zspec(3), pspec, pspec, sspec],
        out_specs=[pl.BlockSpec((tb, d), lambda b, h, t: (b * nt + t, h)), sspec],
        out_shape=[jax.ShapeDtypeStruct((B * T, lay.d_hg), BF16), jax.ShapeDtypeStruct(s0.shape, F32)],
        scratch_shapes=[pltpu.VMEM((nseq, d, d), F32)],
        compiler_params=_params("parallel", "parallel", "arbitrary"),
        name="hgrn",
    )(z, z, z, z, lb, norm_g, s0)


def _rwkv_chunks(r, lw, k, v, a, b, states, C):
    tb, W = r.shape
    nc = tb // C
    C2 = 2 * C
    chunks = range(nc)
    Lc = _block_cumsum(lw, C)
    Lc3 = Lc.reshape(nc, C, W)
    l_end = Lc3[:, C - 1:C, :]
    e_inv = jnp.exp(-Lc)
    e_end = jnp.exp(l_end - Lc3).reshape(tb, W)
    w_end = jnp.exp(l_end)
    in_a = lax.broadcasted_iota(jnp.int32, (1, W), 1) < (W // 2)

    def stacked(x):
        xa = jnp.where(in_a, x, 0.0)
        xb = x - xa
        return [jnp.concatenate([xa[c * C:(c + 1) * C], xb[c * C:(c + 1) * C]], axis=0) for c in chunks]

    Rs = stacked(r * jnp.exp(Lc))
    As = [x.astype(BF16) for x in stacked(a * jnp.exp(Lc - lw))]
    Bs = stacked(b * e_inv)
    Ks = stacked(k * e_inv)
    Bhs = [x.astype(BF16) for x in stacked(b * e_end)]
    Khs = [x.astype(BF16) for x in stacked(k * e_end)]
    Vs = [x.astype(BF16) for x in stacked(v)]

    row = lax.broadcasted_iota(jnp.int32, (C2, C2), 0)
    col = lax.broadcasted_iota(jnp.int32, (C2, C2), 1)
    strict = col < row
    incl = col <= row
    if C2 % LANES == 0:
        sc = [_mm(jnp.concatenate([As[c], Rs[c]], axis=0), jnp.concatenate([Bs[c], Ks[c]], axis=0), _NT)
              for c in chunks]
        s_ab = [x[:C2, :C2] for x in sc]
        s_ak = [x[:C2, C2:] for x in sc]
        s_rb = [x[C2:, :C2] for x in sc]
        s_rk = [x[C2:, C2:] for x in sc]
    else:
        s_ab = [_mm(As[c], Bs[c], _NT) for c in chunks]
        s_ak = [_mm(As[c], Ks[c], _NT) for c in chunks]
        s_rb = [_mm(Rs[c], Bs[c], _NT) for c in chunks]
        s_rk = [_mm(Rs[c], Ks[c], _NT) for c in chunks]
    a_ab = [jnp.where(strict, x, 0.0) for x in s_ab]
    a_ak = [jnp.where(strict, x, 0.0).astype(BF16) for x in s_ak]
    a_rb = [jnp.where(incl, x, 0.0).astype(BF16) for x in s_rb]
    a_rk = [jnp.where(incl, x, 0.0).astype(BF16) for x in s_rk]

    eye = jnp.where(row == col, 1.0, 0.0)
    tinv = [eye + x for x in a_ab]
    pw = [_mm(x, x) for x in a_ab]
    levels = int(math.log2(C)) - 1
    for lvl in range(levels):
        if lvl == levels - 1:
            tinv = [tinv[c] + _mm(pw[c], tinv[c]) for c in chunks]
        elif C2 % LANES == 0:
            both = [_mm(pw[c], jnp.concatenate([pw[c], tinv[c]], axis=1)) for c in chunks]
            pw = [x[:, :C2] for x in both]
            tinv = [tinv[c] + both[c][:, C2:] for c in chunks]
        else:
            tinv = [tinv[c] + _mm(pw[c], tinv[c]) for c in chunks]
            pw = [_mm(x, x) for x in pw]

    av = [_mm(a_ak[c], Vs[c]) for c in chunks]
    pq = [_mm(tinv[c], jnp.concatenate([As[c], av[c].astype(BF16)], axis=1)) for c in chunks]
    pq_b = [x.astype(BF16) for x in pq]
    bp = [_mm(Bhs[c], pq_b[c], _TN) for c in chunks]
    kv = [_mm(Khs[c], Vs[c], _TN) for c in chunks]
    rq = [_mm(a_rb[c], pq_b[c]) for c in chunks]
    ry = [Rs[c] + rq[c][:, :W] for c in chunks]
    yc = [rq[c][:, W:] + _mm(a_rk[c], Vs[c]) for c in chunks]

    rk = lax.broadcasted_iota(jnp.int32, (W, W), 0)
    ck = lax.broadcasted_iota(jnp.int32, (W, W), 1)
    diag = rk == ck
    states = list(states)
    per_seq = nc // len(states)
    starts = []
    for c in chunks:
        s = c // per_seq
        starts.append(states[s])
        M = jnp.where(diag, jnp.broadcast_to(w_end[c], (W, W)), 0.0) + bp[c][:, :W]
        states[s] = _mm(M, states[s]) + (bp[c][:, W:] + kv[c])
    ys = [_mm(ry[c], starts[c]) + yc[c] for c in chunks]
    ys = [x[:C] + x[C:] for x in ys]
    return (ys[0] if nc == 1 else jnp.concatenate(ys, axis=0)), states


def _token_shift(ref, carry, mu_ref, nseq):
    x = ref[...]
    tb, w = x.shape
    tseq = tb // nseq
    seq_start = lax.broadcasted_iota(jnp.int32, (tb, 1), 0) % tseq == 0
    before = jnp.broadcast_to(carry[...], (nseq, tseq, w)).reshape(tb, w)
    prev = jnp.where(seq_start, before, pltpu.roll(x, 1, 0))
    carry[...] = x.reshape(nseq, tseq, w)[:, tseq - 1:tseq, :]
    return x + (prev - x) * mu_ref[...]


def _rwkv_lora_body(lo_ref, mu_ref, s0_ref, w0_ref, w2_ref, a0_ref, a2_ref, g2_ref,
                    lw_ref, a_ref, g_ref, carry, *, nseq, dl_w, dl_a, dl_g):
    @pl.when(pl.program_id(1) == 0)
    def _():
        carry[...] = s0_ref[...]

    lo = _token_shift(lo_ref, carry, mu_ref, nseq)
    wd = lo[:, :dl_w]
    ad = lo[:, dl_w:dl_w + dl_a]
    gd = lo[:, dl_w + dl_a:dl_w + dl_a + dl_g]
    w_log = -jax.nn.softplus(-(w0_ref[...] + _mm_wide(jnp.tanh(wd), w2_ref[...]))) - 0.5
    lw_ref[...] = -jnp.exp(w_log)
    a_ref[...] = jax.nn.sigmoid(a0_ref[...] + _mm(ad, a2_ref[...]))
    g_ref[...] = _mm(jax.nn.sigmoid(gd), g2_ref[...]).astype(g_ref.dtype)


def _rwkv_lora(z, mu_z, shift_z, prm, *, B, T, lay):
    nseq, tb, nt = _seq_tiling(B, T, RECURRENCE_ROWS)
    lw_ = lay.lora_w
    c_lo = lay.off_lora // lw_
    d_rw = lay.d_rw
    dl_w, dl_a, dl_g = prm["w2"].shape[0], prm["a2"].shape[0], prm["g2"].shape[0]
    full = lambda a: pl.BlockSpec(a.shape, lambda b, t: (0, 0))
    out_spec = pl.BlockSpec((tb, d_rw), lambda b, t: (b * nt + t, 0))
    return pl.pallas_call(
        functools.partial(_rwkv_lora_body, nseq=nseq, dl_w=dl_w, dl_a=dl_a, dl_g=dl_g),
        grid=(B // nseq, nt),
        in_specs=[pl.BlockSpec((tb, lw_), lambda b, t: (b * nt + t, c_lo)),
                  pl.BlockSpec((1, lw_), lambda b, t: (0, c_lo)),
                  pl.BlockSpec((nseq, 1, lw_), lambda b, t: (b, 0, c_lo)),
                  full(prm["w0"]), full(prm["w2"]), full(prm["a0"]), full(prm["a2"]), full(prm["g2"])],
        out_specs=[out_spec, out_spec, out_spec],
        out_shape=[jax.ShapeDtypeStruct((B * T, d_rw), F32), jax.ShapeDtypeStruct((B * T, d_rw), F32),
                   jax.ShapeDtypeStruct((B * T, d_rw), BF16)],
        scratch_shapes=[pltpu.VMEM((nseq, 1, lw_), F32)],
        compiler_params=_params("parallel", "arbitrary"),
        name="rwkv_lora",
    )(z, mu_z, shift_z, prm["w0"], prm["w2"], prm["a0"], prm["a2"], prm["g2"])


def _rwkv_body(r_ref, k_ref, v_ref, lw_ref, a_ref, g_ref, mur_ref, muk_ref, muv_ref,
               sr_ref, sk_ref, sv_ref, kk_ref, ka_ref, rk_ref, lng_ref, lnb_ref, h0_ref,
               y_ref, ho_ref, h_sc, cr_sc, ck_sc, cv_sc, *, nseq, chunk, nt):
    t = pl.program_id(2)

    @pl.when(t == 0)
    def _():
        h_sc[...] = h0_ref[...]
        cr_sc[...] = sr_ref[...]
        ck_sc[...] = sk_ref[...]
        cv_sc[...] = sv_ref[...]

    rs = _token_shift(r_ref, cr_sc, mur_ref, nseq)
    ks = _token_shift(k_ref, ck_sc, muk_ref, nseq)
    vs = _token_shift(v_ref, cv_sc, muv_ref, nseq)
    lw = lw_ref[...]
    a_lr = a_ref[...]

    li = lax.broadcasted_iota(jnp.int32, (RW_PAIR, RW_PAIR), 0) // RW_HEAD_DIM
    lj = lax.broadcasted_iota(jnp.int32, (RW_PAIR, RW_PAIR), 1) // RW_HEAD_DIM
    same_head = jnp.where(li == lj, 1.0, 0.0)
    head_sum = lambda x: _mm_mask(x, same_head, terms=2)

    kk = ks * kk_ref[...]
    kk = kk / jnp.maximum(jnp.sqrt(head_sum(kk * kk)), 1e-12)
    k2 = ks * (1.0 + (a_lr - 1.0) * ka_ref[...])

    y, states = _rwkv_chunks(rs, lw, k2, vs, -kk, kk * a_lr, [h_sc[s] for s in range(nseq)], chunk)
    for s in range(nseq):
        h_sc[s] = states[s]

    inv_n = 1.0 / RW_HEAD_DIM
    mu = head_sum(y) * inv_n
    dev = y - mu
    var = head_sum(dev * dev) * inv_n
    yn = dev * lax.rsqrt(var + RW_GN_EPS) * lng_ref[...] + lnb_ref[...]
    bonus = head_sum(rs * k2 * rk_ref[...]) * vs
    y_ref[...] = ((yn + bonus) * g_ref[...].astype(F32)).astype(y_ref.dtype)

    @pl.when(t == nt - 1)
    def _():
        for s in range(nseq):
            ho_ref[s] = states[s]


def _rwkv(z, lw, a_lr, g, mu_z, shift_z, prm, h0, *, B, T, lay):
    npair = lay.d_rw // RW_PAIR
    nseq, tb, nt = _seq_tiling(B, T, RECURRENCE_ROWS)
    w = RW_PAIR
    c_r, c_k, c_v = (lay.off_r // w, lay.off_r // w + npair, lay.off_r // w + 2 * npair)
    zs = lambda c0: pl.BlockSpec((tb, w), lambda b, p, t: (b * nt + t, c0 + p))
    ms = lambda c0: pl.BlockSpec((1, w), lambda b, p, t: (0, c0 + p))
    ss = lambda c0: pl.BlockSpec((nseq, 1, w), lambda b, p, t: (b, 0, c0 + p))
    ps = pl.BlockSpec((1, w), lambda b, p, t: (0, p))
    hs = pl.BlockSpec((nseq, None, w, w), lambda b, p, t: (b, p, 0, 0))
    return pl.pallas_call(
        functools.partial(_rwkv_body, nseq=nseq, chunk=min(RW_CHUNK, tb // nseq), nt=nt),
        grid=(B // nseq, npair, nt),
        in_specs=[
            zs(c_r), zs(c_k), zs(c_v), zs(0), zs(0), zs(0),
            ms(c_r), ms(c_k), ms(c_v),
            ss(c_r), ss(c_k), ss(c_v),
            ps, ps, ps, ps, ps, hs,
        ],
        out_specs=[pl.BlockSpec((tb, w), lambda b, p, t: (b * nt + t, p)), hs],
        out_shape=[jax.ShapeDtypeStruct((B * T, lay.d_rw), BF16), jax.ShapeDtypeStruct(h0.shape, F32)],
        scratch_shapes=[pltpu.VMEM((nseq, w, w), F32), pltpu.VMEM((nseq, 1, w), F32), pltpu.VMEM((nseq, 1, w), F32),
                        pltpu.VMEM((nseq, 1, w), F32)],
        compiler_params=_params("parallel", "parallel", "arbitrary"),
        name="rwkv",
    )(z, z, z, lw, a_lr, g, mu_z, mu_z, mu_z, shift_z, shift_z, shift_z,
      prm["k_k"], prm["k_a"], prm["r_k"], prm["ln_g"], prm["ln_b"], h0)


def _merge_up_body(o_ref, y_ref, ga_ref, gb_ref, wa_ref, wb_ref, m_ref):
    ua = jnp.dot(o_ref[...], wa_ref[...], preferred_element_type=F32)
    ub = jnp.dot(y_ref[...], wb_ref[...], preferred_element_type=F32)
    m = jax.nn.sigmoid(ga_ref[...]) * ua + jax.nn.sigmoid(gb_ref[...]) * ub
    m_ref[...] = m.astype(m_ref.dtype)


def _merge_up(o, y, zg, wa_b, wb_b, *, tm, tn, lay):
    M = o.shape[0]
    D = lay.d_model
    c_ga, c_gb = 0, D // tn
    return pl.pallas_call(
        _merge_up_body,
        grid=(M // tm, D // tn),
        in_specs=[
            pl.BlockSpec((tm, lay.d_hg), lambda i, j: (i, 0)),
            pl.BlockSpec((tm, lay.d_rw), lambda i, j: (i, 0)),
            pl.BlockSpec((tm, tn), lambda i, j: (i, c_ga + j)),
            pl.BlockSpec((tm, tn), lambda i, j: (i, c_gb + j)),
            pl.BlockSpec((lay.d_hg, tn), lambda i, j: (0, j)),
            pl.BlockSpec((lay.d_rw, tn), lambda i, j: (0, j)),
        ],
        out_specs=pl.BlockSpec((tm, tn), lambda i, j: (i, j)),
        out_shape=jax.ShapeDtypeStruct((M, D), BF16),
        compiler_params=_params("parallel", "arbitrary"),
        name="merge_up",
    )(o, y, zg, zg, wa_b, wb_b)


class _ZLayout:
    def __init__(self, d_model, dl_w, dl_a, dl_g):
        self.d_hg = d_model // 2
        self.d_rw = d_model // 2
        self.d_model = d_model
        self.n_lora = dl_w + dl_a + dl_g
        self.off_r = 4 * self.d_hg
        self.off_lora = self.off_r + 3 * self.d_rw
        self.off_gates = self.off_lora + self.n_lora
        self.lora_w = 1024
        assert self.n_lora <= self.lora_w and self.off_lora % self.lora_w == 0
        self.width = self.off_lora + self.lora_w
        self.n_rw_in = 3 * self.d_rw + self.n_lora

    def rw_in_to_z(self, a):
        pads = [(0, 0)] * (a.ndim - 1) + [(self.off_r, self.width - self.off_r - self.n_rw_in)]
        return jnp.pad(a, pads)

    def z_to_rw_in(self, zrow):
        return zrow[..., self.off_r:self.off_r + self.n_rw_in]


def _row_tile(m, pref):
    t = min(pref, m)
    while m % t:
        t //= 2
    return t


def _ffn_weights(w_in, w_down):
    F = w_down.shape[0]
    pad = -F % FF_ALIGN
    wg = jnp.pad(w_in[:, :F], ((0, 0), (0, pad)))
    wu = jnp.pad(w_in[:, F:], ((0, 0), (0, pad)))
    return (jnp.concatenate([wg, wu], axis=1).astype(BF16), jnp.pad(w_down, ((0, pad), (0, 0))).astype(BF16))


def kernel(x_prompt, x_sample, state_hgrn, state_rwkv, state_shift, ln1_g, ln1_b, ffn1_w_in, ffn1_w_down,
           ln2_g, ln2_b, w_in, hg_lb, hg_norm_g, hg_proj, rw_mu, rw_w0, rw_w2, rw_a0, rw_a2, rw_g2, rw_k_k,
           rw_k_a, rw_r_k, rw_ln_g, rw_ln_b, rw_proj, w_out, ln3_g, ln3_b, ffn2_w_in, ffn2_w_down):
    depth = ffn1_w_in.shape[0]
    assert depth == 1, "single-layer stack"
    D = x_prompt.shape[-1]
    alpha = (2 * depth) ** 0.25
    dl_w, dl_a, dl_g = rw_w2.shape[1], rw_a2.shape[1], rw_g2.shape[1]
    assert dl_w % LANES == 0 and dl_a % LANES == 0
    dl_gp = -(-dl_g // LANES) * LANES
    lay = _ZLayout(D, dl_w, dl_a, dl_g)
    nh = lay.d_hg // HG_HEAD_DIM
    nrw = lay.d_rw // RW_HEAD_DIM
    npair = nrw // 2
    l = 0

    row = lambda p: p[l].reshape(1, -1).astype(F32)
    f1_in, f1_dn = _ffn_weights(ffn1_w_in[l], ffn1_w_down[l])
    f2_in, f2_dn = _ffn_weights(ffn2_w_in[l], ffn2_w_down[l])
    w_in_b = w_in[l].astype(BF16)
    w_gates_b = w_in_b[:, lay.off_gates:]
    assert w_gates_b.shape[1] == 2 * D
    wa_b, wb_b, wo_b = hg_proj[l].astype(BF16), rw_proj[l].astype(BF16), w_out[l].astype(BF16)
    lb = jnp.cumsum(jax.nn.softmax(hg_lb.astype(F32), axis=0), axis=0)[l].reshape(1, -1)
    mu_z = lay.rw_in_to_z(rw_mu[l].reshape(1, -1))
    prm = dict(w0=row(rw_w0), w2=rw_w2[l], a0=row(rw_a0), a2=rw_a2[l],
               g2=jnp.pad(rw_g2[l], ((0, dl_gp - dl_g), (0, 0))).astype(BF16),
               k_k=row(rw_k_k), k_a=row(rw_k_a), r_k=row(rw_r_k), ln_g=row(rw_ln_g), ln_b=row(rw_ln_b))

    def ffn(x, xb, w_in_b, w_dn_b, g, b, tm_up, tm_dn, with_bf16):
        act = _gated_up(xb, w_in_b, tm=tm_up, tn=512)
        return _down_ln(act, w_dn_b, x, g, b, alpha=alpha, scale=0.5, tm=tm_dn, tk=FF_ALIGN, with_bf16=with_bf16)

    def trunk(x, hg_s0, rw_s0, shift0):
        B, T, _ = x.shape
        M = B * T
        x0 = x.reshape(M, D)
        tm_up = _row_tile(M, 1024)
        tm_dn = _row_tile(M, 512)
        x1, x1b = ffn(x0, x0.astype(BF16), f1_in, f1_dn, row(ln1_g), row(ln1_b), tm_up, tm_dn, True)
        z = _zproj(x1b, w_in_b, n_cols=lay.width, tm=tm_up, tn=1024)
        zg = _zproj(x1b, w_gates_b, n_cols=2 * D, tm=tm_up, tn=1024)
        o, hg_s = _hgrn(z, lb, row(hg_norm_g), hg_s0, B=B, T=T, lay=lay)
        st = jnp.swapaxes(rw_s0, -1, -2).reshape(B, npair, 2, RW_HEAD_DIM, RW_HEAD_DIM)
        zero = jnp.zeros_like(st[:, :, 0])
        h0 = jnp.concatenate([jnp.concatenate([st[:, :, 0], zero], -1),
                              jnp.concatenate([zero, st[:, :, 1]], -1)], -2)
        shift_z = lay.rw_in_to_z(shift0)
        lw, a_lr, g = _rwkv_lora(z, mu_z, shift_z, prm, B=B, T=T, lay=lay)
        y, h_out = _rwkv(z, lw, a_lr, g, mu_z, shift_z, prm, h0, B=B, T=T, lay=lay)
        hd = RW_HEAD_DIM
        rw_s = jnp.stack([h_out[:, :, :hd, :hd], h_out[:, :, hd:, hd:]], axis=2)
        rw_s = jnp.swapaxes(rw_s.reshape(B, nrw, hd, hd), -1, -2)
        m = _merge_up(o, y, zg, wa_b, wb_b, tm=tm_up, tn=512, lay=lay)
        x2, x2b = _down_ln(m, wo_b, x1, row(ln2_g), row(ln2_b), alpha=alpha, scale=1.0,
                           tm=tm_dn, tk=1024, with_bf16=True)
        (x3,) = ffn(x2, x2b, f2_in, f2_dn, row(ln3_g), row(ln3_b), tm_up, tm_dn, False)
        shift = lay.z_to_rw_in(z.reshape(B, T, -1)[:, -1:, :])
        return x3.reshape(B, T, D), hg_s[None], rw_s[None], shift[None]

    Bp = x_prompt.shape[0]
    hg0 = jnp.zeros((Bp, nh, HG_HEAD_DIM, HG_HEAD_DIM), F32)
    rw0 = jnp.zeros((Bp, nrw, RW_HEAD_DIM, RW_HEAD_DIM), F32)
    sh0 = jnp.zeros((Bp, 1, rw_mu.shape[-1]), F32)
    y_p, hg_p, rw_p, sh_p = trunk(x_prompt, hg0, rw0, sh0)
    y_s, hg_s, rw_s, sh_s = trunk(x_sample, state_hgrn[l].astype(F32), state_rwkv[l].astype(F32),
                                  state_shift[l].astype(F32))
    return (y_p, y_s, hg_p, rw_p, sh_p, hg_s, rw_s, sh_s)
```

```python
import functools
import math

import jax
import jax.numpy as jnp
from jax import lax
from jax.experimental import pallas as pl
from jax.experimental.pallas import tpu as pltpu

F32 = jnp.float32
BF16 = jnp.bfloat16

LANES = 128
HG_HEAD_DIM = 128
RW_HEAD_DIM = 64
RW_PAIR = 2 * RW_HEAD_DIM
HG_BLOCK = 16
RW_CHUNK = 64
RECURRENCE_ROWS = 512
LN_EPS = 1e-5
RMS_EPS = 1e-6
RW_GN_EPS = 64e-5
VMEM_LIMIT = 56 * 1024 * 1024
DOWN_VMEM_LIMIT = 60 * 1024 * 1024
PANEL_ROWS = 32

_NN = (((1,), (0,)), ((), ()))
_NT = (((1,), (1,)), ((), ()))
_TN = (((0,), (0,)), ((), ()))


def _mm(a, b, dims=_NN):
    return lax.dot_general(a.astype(BF16), b.astype(BF16), dims, preferred_element_type=F32)


def _split(x, n):
    parts = []
    for _ in range(n - 1):
        p = x.astype(BF16)
        parts.append(p)
        x = x - p.astype(F32)
    parts.append(x.astype(BF16))
    return parts


def _mm_wide(a, b):
    ah, am = _split(a, 2)
    bh, bm = _split(b, 2)
    dot = lambda p, q: jnp.dot(p, q, preferred_element_type=F32)
    return dot(ah, bh) + (dot(ah, bm) + dot(am, bh))


def _mask_mm(mask, x, terms=3):
    mb = mask.astype(BF16)
    return sum(jnp.dot(mb, p, preferred_element_type=F32) for p in _split(x, terms))


def _mm_mask(x, mask, terms=3):
    mb = mask.astype(BF16)
    return sum(jnp.dot(p, mb, preferred_element_type=F32) for p in _split(x, terms))


def _layer_norm(y, g, b):
    mu = jnp.mean(y, -1, keepdims=True)
    d = y - mu
    var = jnp.mean(d * d, -1, keepdims=True)
    return d * lax.rsqrt(var + LN_EPS) * g + b


def _params(*sem):
    return pltpu.CompilerParams(dimension_semantics=sem, vmem_limit_bytes=VMEM_LIMIT)


def _gated_up_body(x_ref, wg_ref, wu_ref, o_ref):
    x = x_ref[...]
    h = jnp.dot(x, wg_ref[...], preferred_element_type=F32)
    u = jnp.dot(x, wu_ref[...], preferred_element_type=F32)
    o_ref[...] = (h * jax.nn.sigmoid(h) * u).astype(o_ref.dtype)


def _gated_up(xb, w_in_b, *, tm, tn):
    M, D = xb.shape
    F = w_in_b.shape[1] // 2
    nj = F // tn
    return pl.pallas_call(
        _gated_up_body,
        grid=(M // tm, nj),
        in_specs=[pl.BlockSpec((tm, D), lambda i, j: (i, 0)),
                  pl.BlockSpec((D, tn), lambda i, j: (0, j)),
                  pl.BlockSpec((D, tn), lambda i, j: (0, j + nj))],
        out_specs=pl.BlockSpec((tm, tn), lambda i, j: (i, j)),
        out_shape=jax.ShapeDtypeStruct((M, F), BF16),
        compiler_params=_params("parallel", "arbitrary"),
        name="gated_up",
    )(xb, w_in_b, w_in_b)


def _down_ln_body(a_ref, w_ref, x_ref, g_ref, b_ref, o_ref, *maybe_ob_ref, alpha, scale, nj, tn):
    j = pl.program_id(1)
    c0 = pl.multiple_of(j * tn, tn)
    o_ref[:, pl.ds(c0, tn)] = jnp.dot(a_ref[...], w_ref[...], preferred_element_type=F32)

    @pl.when(j == nj - 1)
    def _():
        g = g_ref[...]
        b = b_ref[...]
        nrow = min(PANEL_ROWS, o_ref.shape[0])

        def panel(i, carry):
            r0 = pl.multiple_of(i * nrow, nrow)
            rows = pl.ds(r0, nrow)
            y = _layer_norm(alpha * x_ref[rows, :] + scale * o_ref[rows, :], g, b)
            o_ref[rows, :] = y
            for ob_ref in maybe_ob_ref:
                ob_ref[rows, :] = y.astype(ob_ref.dtype)
            return carry

        lax.fori_loop(0, o_ref.shape[0] // nrow, panel, 0)


def _down_ln(act, w_b, x, ln_g, ln_b, *, alpha, scale, tm, tn, with_bf16):
    M, D = x.shape
    K = act.shape[1]
    nj = D // tn
    once = dict(pipeline_mode=pl.Buffered(1))
    row_spec = pl.BlockSpec((tm, D), lambda i, j: (i, 0))
    out_shape = [jax.ShapeDtypeStruct((M, D), F32)] + ([jax.ShapeDtypeStruct((M, D), BF16)] if with_bf16 else [])
    return pl.pallas_call(
        functools.partial(_down_ln_body, alpha=alpha, scale=scale, nj=nj, tn=tn),
        grid=(M // tm, nj),
        in_specs=[pl.BlockSpec((tm, K), lambda i, j: (i, 0), **once),
                  pl.BlockSpec((K, tn), lambda i, j: (0, j)),
                  pl.BlockSpec((tm, D), lambda i, j: (i, 0), **once),
                  pl.BlockSpec((1, D), lambda i, j: (0, 0)),
                  pl.BlockSpec((1, D), lambda i, j: (0, 0))],
        out_specs=[row_spec] * len(out_shape),
        out_shape=out_shape,
        compiler_params=pltpu.CompilerParams(dimension_semantics=("parallel", "arbitrary"),
                                             vmem_limit_bytes=DOWN_VMEM_LIMIT),
        name="down_ln",
    )(act, w_b, x, ln_g, ln_b)


def _zproj_body(x_ref, w_ref, o_ref):
    o_ref[...] = jnp.dot(x_ref[...], w_ref[...], preferred_element_type=F32)


def _zproj(xb, w_b, *, n_cols, tm, tn):
    M, D = xb.shape
    N = n_cols
    assert N % tn == 0 and N <= w_b.shape[1]
    return pl.pallas_call(
        _zproj_body,
        grid=(M // tm, N // tn),
        in_specs=[pl.BlockSpec((tm, D), lambda i, j: (i, 0)), pl.BlockSpec((D, tn), lambda i, j: (0, j))],
        out_specs=pl.BlockSpec((tm, tn), lambda i, j: (i, j)),
        out_shape=jax.ShapeDtypeStruct((M, N), F32),
        compiler_params=_params("parallel", "arbitrary"),
        name="zproj",
    )(xb, w_b)


def _block_cumsum(x, c):
    n, w = x.shape
    nb = n // c
    ri = lax.broadcasted_iota(jnp.int32, (c, c), 0)
    ci = lax.broadcasted_iota(jnp.int32, (c, c), 1)
    wide = x if nb == 1 else jnp.concatenate([x[i * c:(i + 1) * c, :] for i in range(nb)], axis=1)
    out = _mask_mm(jnp.where(ci <= ri, 1.0, 0.0), wide)
    return out if nb == 1 else jnp.concatenate([out[:, i * w:(i + 1) * w] for i in range(nb)], axis=0)


def _hgrn_body(q_ref, f_ref, v_ref, gh_ref, lb_ref, ng_ref, s0_ref, o_ref, so_ref, st_ref, *, tb, nseq, nt):
    t = pl.program_id(2)
    c = HG_BLOCK
    nb = tb // c
    d = HG_HEAD_DIM

    @pl.when(t == 0)
    def _():
        for s in range(nseq):
            st_ref[s] = s0_ref[s].T

    lb = lb_ref[...]
    q = q_ref[...]
    fp = f_ref[...]
    v = v_ref[...]
    logf = jnp.log(lb + (1.0 - lb) * jax.nn.sigmoid(fp))
    kf = (1.0 - lb) * jax.nn.sigmoid(-fp)
    L = _block_cumsum(logf, c)
    L3, q3, kf3, v3 = (x.reshape(nb, c, d) for x in (L, q, kf, v))
    l_end = L3[:, c - 1:c, :]
    qe = q * jnp.exp(L)
    ke = (kf3 * jnp.exp(l_end - L3)).reshape(tb, d)
    d_end = jnp.exp(l_end)

    h = c // 2
    rows = lax.broadcasted_iota(jnp.int32, (1, h, 1), 1)
    halves = [(L3[:, :h, :], q3[:, :h, :]), (L3[:, h:, :], q3[:, h:, :])]
    acc = [jnp.zeros((nb, h, d), F32), jnp.zeros((nb, h, d), F32)]
    for s in range(c):
        l_s, k_s, v_s = L3[:, s:s + 1, :], kf3[:, s:s + 1, :], v3[:, s:s + 1, :]
        for half, (l_t, q_t) in enumerate(halves):
            first = s - half * h
            if first >= h:
                continue
            diff = l_t - l_s
            if first > 0:
                diff = jnp.where(rows >= first, diff, -jnp.inf)
            sc = jnp.sum(jnp.exp(diff) * q_t * k_s, -1, keepdims=True)
            acc[half] = acc[half] + sc * v_s
    o = jnp.concatenate(acc, axis=1).reshape(tb, d)

    blk = lambda x, i: x[i * c:(i + 1) * c, :]
    kv = [_mm(blk(v, i), blk(ke, i), _TN) for i in range(nb)]
    per_seq = nb // nseq
    states = [st_ref[s] for s in range(nseq)]
    starts = []
    for i in range(nb):
        s = i // per_seq
        starts.append(states[s])
        states[s] = states[s] * d_end[i] + kv[i]
    for s in range(nseq):
        st_ref[s] = states[s]
    o = o + jnp.concatenate([_mm(blk(qe, i), starts[i], _NT) for i in range(nb)], axis=0)

    o = o * lax.rsqrt(jnp.mean(o * o, -1, keepdims=True) + RMS_EPS)
    gh = gh_ref[...]
    o_ref[...] = (o * ng_ref[...] * (gh * jax.nn.sigmoid(gh))).astype(o_ref.dtype)

    @pl.when(t == nt - 1)
    def _():
        for s in range(nseq):
            so_ref[s] = states[s].T


def _seq_tiling(B, T, rows):
    if T >= rows:
        tb = _row_tile(T, rows)
        return 1, tb, T // tb
    nseq = _row_tile(B, max(1, rows // T))
    return nseq, nseq * T, 1


def _hgrn(z, lb, norm_g, s0, *, B, T, lay):
    nh = lay.d_hg // HG_HEAD_DIM
    nseq, tb, nt = _seq_tiling(B, T, RECURRENCE_ROWS)
    d = HG_HEAD_DIM
    zspec = lambda sec: pl.BlockSpec((tb, d), lambda b, h, t: (b * nt + t, sec * nh + h))
    pspec = pl.BlockSpec((1, d), lambda b, h, t: (0, h))
    sspec = pl.BlockSpec((nseq, None, d, d), lambda b, h, t: (b, h, 0, 0))
    return pl.pallas_call(
        functools.partial(_hgrn_body, tb=tb, nseq=nseq, nt=nt),
        grid=(B // nseq, nh, nt),
        in_specs=[zspec(0), zspec(1), zspec(2), zspec(3), pspec, pspec, sspec],
        out_specs=[pl.BlockSpec((tb, d), lambda b, h, t: (b * nt + t, h)), sspec],
        out_shape=[jax.ShapeDtypeStruct((B * T, lay.d_hg), BF16), jax.ShapeDtypeStruct(s0.shape, F32)],
        scratch_shapes=[pltpu.VMEM((nseq, d, d), F32)],
        compiler_params=_params("parallel", "parallel", "arbitrary"),
        name="hgrn",
    )(z, z, z, z, lb, norm_g, s0)


def _rwkv_chunks(r, lw, k, v, a, b, states, C):
    tb, W = r.shape
    nc = tb // C
    C2 = 2 * C
    chunks = range(nc)
    Lc = _block_cumsum(lw, C)
    Lc3 = Lc.reshape(nc, C, W)
    l_end = Lc3[:, C - 1:C, :]
    e_inv = jnp.exp(-Lc)
    e_end = jnp.exp(l_end - Lc3).reshape(tb, W)
    w_end = jnp.exp(l_end)
    in_a = lax.broadcasted_iota(jnp.int32, (1, W), 1) < (W // 2)

    def stacked(x):
        xa = jnp.where(in_a, x, 0.0)
        xb = x - xa
        return [jnp.concatenate([xa[c * C:(c + 1) * C], xb[c * C:(c + 1) * C]], axis=0) for c in chunks]

    Rs = stacked(r * jnp.exp(Lc))
    As = [x.astype(BF16) for x in stacked(a * jnp.exp(Lc - lw))]
    Bs = stacked(b * e_inv)
    Ks = stacked(k * e_inv)
    Bhs = [x.astype(BF16) for x in stacked(b * e_end)]
    Khs = [x.astype(BF16) for x in stacked(k * e_end)]
    Vs = [x.astype(BF16) for x in stacked(v)]

    row = lax.broadcasted_iota(jnp.int32, (C2, C2), 0)
    col = lax.broadcasted_iota(jnp.int32, (C2, C2), 1)
    strict = col < row
    incl = col <= row
    if C2 % LANES == 0:
        sc = [_mm(jnp.concatenate([As[c], Rs[c]], axis=0), jnp.concatenate([Bs[c], Ks[c]], axis=0), _NT)
              for c in chunks]
        s_ab = [x[:C2, :C2] for x in sc]
        s_ak = [x[:C2, C2:] for x in sc]
        s_rb = [x[C2:, :C2] for x in sc]
        s_rk = [x[C2:, C2:] for x in sc]
    else:
        s_ab = [_mm(As[c], Bs[c], _NT) for c in chunks]
        s_ak = [_mm(As[c], Ks[c], _NT) for c in chunks]
        s_rb = [_mm(Rs[c], Bs[c], _NT) for c in chunks]
        s_rk = [_mm(Rs[c], Ks[c], _NT) for c in chunks]
    a_ab = [jnp.where(strict, x, 0.0) for x in s_ab]
    a_ak = [jnp.where(strict, x, 0.0).astype(BF16) for x in s_ak]
    a_rb = [jnp.where(incl, x, 0.0).astype(BF16) for x in s_rb]
    a_rk = [jnp.where(incl, x, 0.0).astype(BF16) for x in s_rk]

    eye = jnp.where(row == col, 1.0, 0.0)
    tinv = [eye + x for x in a_ab]
    pw = [_mm(x, x) for x in a_ab]
    levels = int(math.log2(C)) - 1
    for lvl in range(levels):
        if lvl == levels - 1:
            tinv = [tinv[c] + _mm(pw[c], tinv[c]) for c in chunks]
        elif C2 % LANES == 0:
            both = [_mm(pw[c], jnp.concatenate([pw[c], tinv[c]], axis=1)) for c in chunks]
            pw = [x[:, :C2] for x in both]
            tinv = [tinv[c] + both[c][:, C2:] for c in chunks]
        else:
            tinv = [tinv[c] + _mm(pw[c], tinv[c]) for c in chunks]
            pw = [_mm(x, x) for x in pw]

    av = [_mm(a_ak[c], Vs[c]) for c in chunks]
    pq = [_mm(tinv[c], jnp.concatenate([As[c], av[c].astype(BF16)], axis=1)) for c in chunks]
    pq_b = [x.astype(BF16) for x in pq]
    bp = [_mm(Bhs[c], pq_b[c], _TN) for c in chunks]
    kv = [_mm(Khs[c], Vs[c], _TN) for c in chunks]
    rq = [_mm(a_rb[c], pq_b[c]) for c in chunks]
    ry = [Rs[c] + rq[c][:, :W] for c in chunks]
    yc = [rq[c][:, W:] + _mm(a_rk[c], Vs[c]) for c in chunks]

    rk = lax.broadcasted_iota(jnp.int32, (W, W), 0)
    ck = lax.broadcasted_iota(jnp.int32, (W, W), 1)
    diag = rk == ck
    states = list(states)
    per_seq = nc // len(states)
    starts = []
    for c in chunks:
        s = c // per_seq
        starts.append(states[s])
        M = jnp.where(diag, jnp.broadcast_to(w_end[c], (W, W)), 0.0) + bp[c][:, :W]
        states[s] = _mm(M, states[s]) + (bp[c][:, W:] + kv[c])
    ys = [_mm(ry[c], starts[c]) + yc[c] for c in chunks]
    ys = [x[:C] + x[C:] for x in ys]
    return (ys[0] if nc == 1 else jnp.concatenate(ys, axis=0)), states


def _token_shift(ref, carry, mu_ref, nseq):
    x = ref[...]
    tb, w = x.shape
    tseq = tb // nseq
    seq_start = lax.broadcasted_iota(jnp.int32, (tb, 1), 0) % tseq == 0
    before = jnp.broadcast_to(carry[...], (nseq, tseq, w)).reshape(tb, w)
    prev = jnp.where(seq_start, before, pltpu.roll(x, 1, 0))
    carry[...] = x.reshape(nseq, tseq, w)[:, tseq - 1:tseq, :]
    return x + (prev - x) * mu_ref[...]


def _rwkv_lora_body(lo_ref, mu_ref, s0_ref, w0_ref, w2_ref, a0_ref, a2_ref, g2_ref,
                    lw_ref, a_ref, g_ref, carry, *, nseq, dl_w, dl_a, dl_g):
    @pl.when(pl.program_id(1) == 0)
    def _():
        carry[...] = s0_ref[...]

    lo = _token_shift(lo_ref, carry, mu_ref, nseq)
    wd = lo[:, :dl_w]
    ad = lo[:, dl_w:dl_w + dl_a]
    gd = lo[:, dl_w + dl_a:dl_w + dl_a + dl_g]
    w_log = -jax.nn.softplus(-(w0_ref[...] + _mm_wide(jnp.tanh(wd), w2_ref[...]))) - 0.5
    lw_ref[...] = -jnp.exp(w_log)
    a_ref[...] = jax.nn.sigmoid(a0_ref[...] + _mm(ad, a2_ref[...]))
    g_ref[...] = _mm(jax.nn.sigmoid(gd), g2_ref[...]).astype(g_ref.dtype)


def _rwkv_lora(z, mu_z, shift_z, prm, *, B, T, lay):
    nseq, tb, nt = _seq_tiling(B, T, RECURRENCE_ROWS)
    lw_ = lay.lora_w
    c_lo = lay.off_lora // lw_
    d_rw = lay.d_rw
    dl_w, dl_a, dl_g = prm["w2"].shape[0], prm["a2"].shape[0], prm["g2"].shape[0]
    full = lambda a: pl.BlockSpec(a.shape, lambda b, t: (0, 0))
    out_spec = pl.BlockSpec((tb, d_rw), lambda b, t: (b * nt + t, 0))
    return pl.pallas_call(
        functools.partial(_rwkv_lora_body, nseq=nseq, dl_w=dl_w, dl_a=dl_a, dl_g=dl_g),
        grid=(B // nseq, nt),
        in_specs=[pl.BlockSpec((tb, lw_), lambda b, t: (b * nt + t, c_lo)),
                  pl.BlockSpec((1, lw_), lambda b, t: (0, c_lo)),
                  pl.BlockSpec((nseq, 1, lw_), lambda b, t: (b, 0, c_lo)),
                  full(prm["w0"]), full(prm["w2"]), full(prm["a0"]), full(prm["a2"]), full(prm["g2"])],
        out_specs=[out_spec, out_spec, out_spec],
        out_shape=[jax.ShapeDtypeStruct((B * T, d_rw), F32), jax.ShapeDtypeStruct((B * T, d_rw), F32),
                   jax.ShapeDtypeStruct((B * T, d_rw), BF16)],
        scratch_shapes=[pltpu.VMEM((nseq, 1, lw_), F32)],
        compiler_params=_params("parallel", "arbitrary"),
        name="rwkv_lora",
    )(z, mu_z, shift_z, prm["w0"], prm["w2"], prm["a0"], prm["a2"], prm["g2"])


def _rwkv_body(r_ref, k_ref, v_ref, lw_ref, a_ref, g_ref, mur_ref, muk_ref, muv_ref,
               sr_ref, sk_ref, sv_ref, kk_ref, ka_ref, rk_ref, lng_ref, lnb_ref, h0_ref,
               y_ref, ho_ref, h_sc, cr_sc, ck_sc, cv_sc, *, nseq, chunk, nt):
    t = pl.program_id(2)

    @pl.when(t == 0)
    def _():
        h_sc[...] = h0_ref[...]
        cr_sc[...] = sr_ref[...]
        ck_sc[...] = sk_ref[...]
        cv_sc[...] = sv_ref[...]

    rs = _token_shift(r_ref, cr_sc, mur_ref, nseq)
    ks = _token_shift(k_ref, ck_sc, muk_ref, nseq)
    vs = _token_shift(v_ref, cv_sc, muv_ref, nseq)
    lw = lw_ref[...]
    a_lr = a_ref[...]

    li = lax.broadcasted_iota(jnp.int32, (RW_PAIR, RW_PAIR), 0) // RW_HEAD_DIM
    lj = lax.broadcasted_iota(jnp.int32, (RW_PAIR, RW_PAIR), 1) // RW_HEAD_DIM
    same_head = jnp.where(li == lj, 1.0, 0.0)
    head_sum = lambda x: _mm_mask(x, same_head, terms=2)

    kk = ks * kk_ref[...]
    kk = kk / jnp.maximum(jnp.sqrt(head_sum(kk * kk)), 1e-12)
    k2 = ks * (1.0 + (a_lr - 1.0) * ka_ref[...])

    y, states = _rwkv_chunks(rs, lw, k2, vs, -kk, kk * a_lr, [h_sc[s] for s in range(nseq)], chunk)
    for s in range(nseq):
        h_sc[s] = states[s]

    inv_n = 1.0 / RW_HEAD_DIM
    mu = head_sum(y) * inv_n
    dev = y - mu
    var = head_sum(dev * dev) * inv_n
    yn = dev * lax.rsqrt(var + RW_GN_EPS) * lng_ref[...] + lnb_ref[...]
    bonus = head_sum(rs * k2 * rk_ref[...]) * vs
    y_ref[...] = ((yn + bonus) * g_ref[...].astype(F32)).astype(y_ref.dtype)

    @pl.when(t == nt - 1)
    def _():
        for s in range(nseq):
            ho_ref[s] = states[s]


def _rwkv(z, lw, a_lr, g, mu_z, shift_z, prm, h0, *, B, T, lay):
    npair = lay.d_rw // RW_PAIR
    nseq, tb, nt = _seq_tiling(B, T, RECURRENCE_ROWS)
    w = RW_PAIR
    c_r, c_k, c_v = (lay.off_r // w, lay.off_r // w + npair, lay.off_r // w + 2 * npair)
    zs = lambda c0: pl.BlockSpec((tb, w), lambda b, p, t: (b * nt + t, c0 + p))
    ms = lambda c0: pl.BlockSpec((1, w), lambda b, p, t: (0, c0 + p))
    ss = lambda c0: pl.BlockSpec((nseq, 1, w), lambda b, p, t: (b, 0, c0 + p))
    ps = pl.BlockSpec((1, w), lambda b, p, t: (0, p))
    hs = pl.BlockSpec((nseq, None, w, w), lambda b, p, t: (b, p, 0, 0))
    return pl.pallas_call(
        functools.partial(_rwkv_body, nseq=nseq, chunk=min(RW_CHUNK, tb // nseq), nt=nt),
        grid=(B // nseq, npair, nt),
        in_specs=[
            zs(c_r), zs(c_k), zs(c_v), zs(0), zs(0), zs(0),
            ms(c_r), ms(c_k), ms(c_v),
            ss(c_r), ss(c_k), ss(c_v),
            ps, ps, ps, ps, ps, hs,
        ],
        out_specs=[pl.BlockSpec((tb, w), lambda b, p, t: (b * nt + t, p)), hs],
        out_shape=[jax.ShapeDtypeStruct((B * T, lay.d_rw), BF16), jax.ShapeDtypeStruct(h0.shape, F32)],
        scratch_shapes=[pltpu.VMEM((nseq, w, w), F32), pltpu.VMEM((nseq, 1, w), F32), pltpu.VMEM((nseq, 1, w), F32),
                        pltpu.VMEM((nseq, 1, w), F32)],
        compiler_params=_params("parallel", "parallel", "arbitrary"),
        name="rwkv",
    )(z, z, z, lw, a_lr, g, mu_z, mu_z, mu_z, shift_z, shift_z, shift_z,
      prm["k_k"], prm["k_a"], prm["r_k"], prm["ln_g"], prm["ln_b"], h0)


def _merge_up_body(o_ref, y_ref, ga_ref, gb_ref, wa_ref, wb_ref, m_ref):
    ua = jnp.dot(o_ref[...], wa_ref[...], preferred_element_type=F32)
    ub = jnp.dot(y_ref[...], wb_ref[...], preferred_element_type=F32)
    m = jax.nn.sigmoid(ga_ref[...]) * ua + jax.nn.sigmoid(gb_ref[...]) * ub
    m_ref[...] = m.astype(m_ref.dtype)


def _merge_up(o, y, zg, wa_b, wb_b, *, tm, tn, lay):
    M = o.shape[0]
    D = lay.d_model
    c_ga, c_gb = 0, D // tn
    return pl.pallas_call(
        _merge_up_body,
        grid=(M // tm, D // tn),
        in_specs=[
            pl.BlockSpec((tm, lay.d_hg), lambda i, j: (i, 0)),
            pl.BlockSpec((tm, lay.d_rw), lambda i, j: (i, 0)),
            pl.BlockSpec((tm, tn), lambda i, j: (i, c_ga + j)),
            pl.BlockSpec((tm, tn), lambda i, j: (i, c_gb + j)),
            pl.BlockSpec((lay.d_hg, tn), lambda i, j: (0, j)),
            pl.BlockSpec((lay.d_rw, tn), lambda i, j: (0, j)),
        ],
        out_specs=pl.BlockSpec((tm, tn), lambda i, j: (i, j)),
        out_shape=jax.ShapeDtypeStruct((M, D), BF16),
        compiler_params=_params("parallel", "arbitrary"),
        name="merge_up",
    )(o, y, zg, zg, wa_b, wb_b)


class _ZLayout:
    def __init__(self, d_model, dl_w, dl_a, dl_g):
        self.d_hg = d_model // 2
        self.d_rw = d_model // 2
        self.d_model = d_model
        self.n_lora = dl_w + dl_a + dl_g
        self.off_r = 4 * self.d_hg
        self.off_lora = self.off_r + 3 * self.d_rw
        self.off_gates = self.off_lora + self.n_lora
        self.lora_w = 1024
        assert self.n_lora <= self.lora_w and self.off_lora % self.lora_w == 0
        self.width = self.off_lora + self.lora_w
        self.n_rw_in = 3 * self.d_rw + self.n_lora

    def rw_in_to_z(self, a):
        pads = [(0, 0)] * (a.ndim - 1) + [(self.off_r, self.width - self.off_r - self.n_rw_in)]
        return jnp.pad(a, pads)

    def z_to_rw_in(self, zrow):
        return zrow[..., self.off_r:self.off_r + self.n_rw_in]


def _row_tile(m, pref):
    t = min(pref, m)
    while m % t:
        t //= 2
    return t


def kernel(x_prompt, x_sample, state_hgrn, state_rwkv, state_shift, ln1_g, ln1_b, ffn1_w_in, ffn1_w_down,
           ln2_g, ln2_b, w_in, hg_lb, hg_norm_g, hg_proj, rw_mu, rw_w0, rw_w2, rw_a0, rw_a2, rw_g2, rw_k_k,
           rw_k_a, rw_r_k, rw_ln_g, rw_ln_b, rw_proj, w_out, ln3_g, ln3_b, ffn2_w_in, ffn2_w_down):
    depth = ffn1_w_in.shape[0]
    assert depth == 1, "single-layer stack"
    D = x_prompt.shape[-1]
    alpha = (2 * depth) ** 0.25
    dl_w, dl_a, dl_g = rw_w2.shape[1], rw_a2.shape[1], rw_g2.shape[1]
    assert dl_w % LANES == 0 and dl_a % LANES == 0
    dl_gp = -(-dl_g // LANES) * LANES
    lay = _ZLayout(D, dl_w, dl_a, dl_g)
    nh = lay.d_hg // HG_HEAD_DIM
    nrw = lay.d_rw // RW_HEAD_DIM
    npair = nrw // 2
    l = 0

    row = lambda p: p[l].reshape(1, -1).astype(F32)
    f1_in, f1_dn = ffn1_w_in[l].astype(BF16), ffn1_w_down[l].astype(BF16)
    f2_in, f2_dn = ffn2_w_in[l].astype(BF16), ffn2_w_down[l].astype(BF16)
    w_in_b = w_in[l].astype(BF16)
    w_gates_b = w_in_b[:, lay.off_gates:]
    assert w_gates_b.shape[1] == 2 * D
    wa_b, wb_b, wo_b = hg_proj[l].astype(BF16), rw_proj[l].astype(BF16), w_out[l].astype(BF16)
    lb = jnp.cumsum(jax.nn.softmax(hg_lb.astype(F32), axis=0), axis=0)[l].reshape(1, -1)
    mu_z = lay.rw_in_to_z(rw_mu[l].reshape(1, -1))
    prm = dict(w0=row(rw_w0), w2=rw_w2[l], a0=row(rw_a0), a2=rw_a2[l],
               g2=jnp.pad(rw_g2[l], ((0, dl_gp - dl_g), (0, 0))).astype(BF16),
               k_k=row(rw_k_k), k_a=row(rw_k_a), r_k=row(rw_r_k), ln_g=row(rw_ln_g), ln_b=row(rw_ln_b))

    def ffn(x, xb, w_in_b, w_dn_b, g, b, tm_dn, with_bf16):
        ff = w_dn_b.shape[0]
        tn_up = max(t for t in (LANES, 2 * LANES, 4 * LANES) if ff % t == 0)
        act = _gated_up(xb, w_in_b, tm=_row_tile(x.shape[0], 512 * 1024 // tn_up), tn=tn_up)
        return _down_ln(act, w_dn_b, x, g, b, alpha=alpha, scale=0.5, tm=tm_dn, tn=256, with_bf16=with_bf16)

    def trunk(x, hg_s0, rw_s0, shift0):
        B, T, _ = x.shape
        M = B * T
        x0 = x.reshape(M, D)
        tm_up = _row_tile(M, 1024)
        tm_dn = _row_tile(M, 512)
        x1, x1b = ffn(x0, x0.astype(BF16), f1_in, f1_dn, row(ln1_g), row(ln1_b), tm_dn, True)
        z = _zproj(x1b, w_in_b, n_cols=lay.width, tm=tm_up, tn=1024)
        zg = _zproj(x1b, w_gates_b, n_cols=2 * D, tm=tm_up, tn=1024)
        o, hg_s = _hgrn(z, lb, row(hg_norm_g), hg_s0, B=B, T=T, lay=lay)
        st = jnp.swapaxes(rw_s0, -1, -2).reshape(B, npair, 2, RW_HEAD_DIM, RW_HEAD_DIM)
        zero = jnp.zeros_like(st[:, :, 0])
        h0 = jnp.concatenate([jnp.concatenate([st[:, :, 0], zero], -1),
                              jnp.concatenate([zero, st[:, :, 1]], -1)], -2)
        shift_z = lay.rw_in_to_z(shift0)
        lw, a_lr, g = _rwkv_lora(z, mu_z, shift_z, prm, B=B, T=T, lay=lay)
        y, h_out = _rwkv(z, lw, a_lr, g, mu_z, shift_z, prm, h0, B=B, T=T, lay=lay)
        hd = RW_HEAD_DIM
        rw_s = jnp.stack([h_out[:, :, :hd, :hd], h_out[:, :, hd:, hd:]], axis=2)
        rw_s = jnp.swapaxes(rw_s.reshape(B, nrw, hd, hd), -1, -2)
        m = _merge_up(o, y, zg, wa_b, wb_b, tm=tm_up, tn=512, lay=lay)
        x2, x2b = _down_ln(m, wo_b, x1, row(ln2_g), row(ln2_b), alpha=alpha, scale=1.0,
                           tm=tm_dn, tn=256, with_bf16=True)
        (x3,) = ffn(x2, x2b, f2_in, f2_dn, row(ln3_g), row(ln3_b), tm_dn, False)
        shift = lay.z_to_rw_in(z.reshape(B, T, -1)[:, -1:, :])
        return x3.reshape(B, T, D), hg_s[None], rw_s[None], shift[None]

    Bp = x_prompt.shape[0]
    hg0 = jnp.zeros((Bp, nh, HG_HEAD_DIM, HG_HEAD_DIM), F32)
    rw0 = jnp.zeros((Bp, nrw, RW_HEAD_DIM, RW_HEAD_DIM), F32)
    sh0 = jnp.zeros((Bp, 1, rw_mu.shape[-1]), F32)
    y_p, hg_p, rw_p, sh_p = trunk(x_prompt, hg0, rw0, sh0)
    y_s, hg_s, rw_s, sh_s = trunk(x_sample, state_hgrn[l].astype(F32), state_rwkv[l].astype(F32),
                                  state_shift[l].astype(F32))
    return (y_p, y_s, hg_p, rw_p, sh_p, hg_s, rw_s, sh_s)
```

```python
import functools
import math

import jax
import jax.numpy as jnp
from jax import lax
from jax.experimental import pallas as pl
from jax.experimental.pallas import tpu as pltpu

F32 = jnp.float32
BF16 = jnp.bfloat16

LANES = 128
HG_HEAD_DIM = 128
RW_HEAD_DIM = 64
RW_PAIR = 2 * RW_HEAD_DIM
HG_BLOCK = 16
RW_CHUNK = 64
RECURRENCE_ROWS = 512
LN_EPS = 1e-5
RMS_EPS = 1e-6
RW_GN_EPS = 64e-5
VMEM_LIMIT = 56 * 1024 * 1024
DOWN_VMEM_LIMIT = 60 * 1024 * 1024
PANEL_ROWS = 32
FFN_DOWN_COLS = 256
OUT_PROJ_COLS = 1024

_NN = (((1,), (0,)), ((), ()))
_NT = (((1,), (1,)), ((), ()))
_TN = (((0,), (0,)), ((), ()))


def _mm(a, b, dims=_NN):
    return lax.dot_general(a.astype(BF16), b.astype(BF16), dims, preferred_element_type=F32)


def _split(x, n):
    parts = []
    for _ in range(n - 1):
        p = x.astype(BF16)
        parts.append(p)
        x = x - p.astype(F32)
    parts.append(x.astype(BF16))
    return parts


def _mm_wide(a, b):
    ah, am = _split(a, 2)
    bh, bm = _split(b, 2)
    dot = lambda p, q: jnp.dot(p, q, preferred_element_type=F32)
    return dot(ah, bh) + (dot(ah, bm) + dot(am, bh))


def _mask_mm(mask, x, terms=3):
    mb = mask.astype(BF16)
    return sum(jnp.dot(mb, p, preferred_element_type=F32) for p in _split(x, terms))


def _layer_norm(y, g, b):
    mu = jnp.mean(y, -1, keepdims=True)
    d = y - mu
    var = jnp.mean(d * d, -1, keepdims=True)
    return d * lax.rsqrt(var + LN_EPS) * g + b


def _params(*sem):
    return pltpu.CompilerParams(dimension_semantics=sem, vmem_limit_bytes=VMEM_LIMIT)


def _gated_up_body(x_ref, wg_ref, wu_ref, o_ref):
    x = x_ref[...]
    h = jnp.dot(x, wg_ref[...], preferred_element_type=F32)
    u = jnp.dot(x, wu_ref[...], preferred_element_type=F32)
    o_ref[...] = (h * jax.nn.sigmoid(h) * u).astype(o_ref.dtype)


def _gated_up(xb, w_in_b, *, tm, tn):
    M, D = xb.shape
    F = w_in_b.shape[1] // 2
    nj = F // tn
    return pl.pallas_call(
        _gated_up_body,
        grid=(M // tm, nj),
        in_specs=[pl.BlockSpec((tm, D), lambda i, j: (i, 0)),
                  pl.BlockSpec((D, tn), lambda i, j: (0, j)),
                  pl.BlockSpec((D, tn), lambda i, j: (0, j + nj))],
        out_specs=pl.BlockSpec((tm, tn), lambda i, j: (i, j)),
        out_shape=jax.ShapeDtypeStruct((M, F), BF16),
        compiler_params=_params("parallel", "arbitrary"),
        name="gated_up",
    )(xb, w_in_b, w_in_b)


def _down_ln_body(a_ref, w_ref, x_ref, g_ref, b_ref, o_ref, *maybe_ob_ref, alpha, scale, nj, tn):
    j = pl.program_id(1)
    c0 = pl.multiple_of(j * tn, tn)
    o_ref[:, pl.ds(c0, tn)] = jnp.dot(a_ref[...], w_ref[...], preferred_element_type=F32)

    @pl.when(j == nj - 1)
    def _():
        g = g_ref[...]
        b = b_ref[...]
        nrow = min(PANEL_ROWS, o_ref.shape[0])

        def panel(i, carry):
            r0 = pl.multiple_of(i * nrow, nrow)
            rows = pl.ds(r0, nrow)
            y = _layer_norm(alpha * x_ref[rows, :] + scale * o_ref[rows, :], g, b)
            o_ref[rows, :] = y
            for ob_ref in maybe_ob_ref:
                ob_ref[rows, :] = y.astype(ob_ref.dtype)
            return carry

        lax.fori_loop(0, o_ref.shape[0] // nrow, panel, 0)


def _column_panels(w, tn):
    K, N = w.shape
    return w.reshape(K, N // tn, tn).transpose(1, 0, 2)


def _down_ln(act, w_panels, x, ln_g, ln_b, *, alpha, scale, tm, with_bf16):
    M, D = x.shape
    nj, K, tn = w_panels.shape
    assert K == act.shape[1] and nj * tn == D
    once = dict(pipeline_mode=pl.Buffered(1))
    row_spec = pl.BlockSpec((tm, D), lambda i, j: (i, 0))
    out_shape = [jax.ShapeDtypeStruct((M, D), F32)] + ([jax.ShapeDtypeStruct((M, D), BF16)] if with_bf16 else [])
    return pl.pallas_call(
        functools.partial(_down_ln_body, alpha=alpha, scale=scale, nj=nj, tn=tn),
        grid=(M // tm, nj),
        in_specs=[pl.BlockSpec((tm, K), lambda i, j: (i, 0), **once),
                  pl.BlockSpec((None, K, tn), lambda i, j: (j, 0, 0)),
                  pl.BlockSpec((tm, D), lambda i, j: (i, 0), **once),
                  pl.BlockSpec((1, D), lambda i, j: (0, 0)),
                  pl.BlockSpec((1, D), lambda i, j: (0, 0))],
        out_specs=[row_spec] * len(out_shape),
        out_shape=out_shape,
        compiler_params=pltpu.CompilerParams(dimension_semantics=("parallel", "arbitrary"),
                                             vmem_limit_bytes=DOWN_VMEM_LIMIT),
        name="down_ln",
    )(act, w_panels, x, ln_g, ln_b)


def _zproj_body(x_ref, w_ref, o_ref):
    o_ref[...] = jnp.dot(x_ref[...], w_ref[...], preferred_element_type=F32)


def _zproj(xb, w_b, *, n_cols, tm, tn):
    M, D = xb.shape
    N = n_cols
    assert N % tn == 0 and N <= w_b.shape[1]
    return pl.pallas_call(
        _zproj_body,
        grid=(M // tm, N // tn),
        in_specs=[pl.BlockSpec((tm, D), lambda i, j: (i, 0)), pl.BlockSpec((D, tn), lambda i, j: (0, j))],
        out_specs=pl.BlockSpec((tm, tn), lambda i, j: (i, j)),
        out_shape=jax.ShapeDtypeStruct((M, N), F32),
        compiler_params=_params("parallel", "arbitrary"),
        name="zproj",
    )(xb, w_b)


def _block_cumsum(x, c):
    n, w = x.shape
    nb = n // c
    ri = lax.broadcasted_iota(jnp.int32, (c, c), 0)
    ci = lax.broadcasted_iota(jnp.int32, (c, c), 1)
    wide = x if nb == 1 else jnp.concatenate([x[i * c:(i + 1) * c, :] for i in range(nb)], axis=1)
    out = _mask_mm(jnp.where(ci <= ri, 1.0, 0.0), wide)
    return out if nb == 1 else jnp.concatenate([out[:, i * w:(i + 1) * w] for i in range(nb)], axis=0)


def _hgrn_body(q_ref, f_ref, v_ref, gh_ref, lb_ref, ng_ref, s0_ref, o_ref, so_ref, st_ref, *, tb, nseq, nt):
    t = pl.program_id(2)
    c = HG_BLOCK
    nb = tb // c
    d = HG_HEAD_DIM

    @pl.when(t == 0)
    def _():
        for s in range(nseq):
            st_ref[s] = s0_ref[s].T

    lb = lb_ref[...]
    q = q_ref[...]
    fp = f_ref[...]
    v = v_ref[...]
    logf = jnp.log(lb + (1.0 - lb) * jax.nn.sigmoid(fp))
    kf = (1.0 - lb) * jax.nn.sigmoid(-fp)
    L = _block_cumsum(logf, c)
    L3, q3, kf3, v3 = (x.reshape(nb, c, d) for x in (L, q, kf, v))
    l_end = L3[:, c - 1:c, :]
    qe = q * jnp.exp(L)
    ke = (kf3 * jnp.exp(l_end - L3)).reshape(tb, d)
    d_end = jnp.exp(l_end)

    h = c // 2
    rows = lax.broadcasted_iota(jnp.int32, (1, h, 1), 1)
    halves = [(L3[:, :h, :], q3[:, :h, :]), (L3[:, h:, :], q3[:, h:, :])]
    acc = [jnp.zeros((nb, h, d), F32), jnp.zeros((nb, h, d), F32)]
    for s in range(c):
        l_s, k_s, v_s = L3[:, s:s + 1, :], kf3[:, s:s + 1, :], v3[:, s:s + 1, :]
        for half, (l_t, q_t) in enumerate(halves):
            first = s - half * h
            if first >= h:
                continue
            diff = l_t - l_s
            if first > 0:
                diff = jnp.where(rows >= first, diff, -jnp.inf)
            sc = jnp.sum(jnp.exp(diff) * q_t * k_s, -1, keepdims=True)
            acc[half] = acc[half] + sc * v_s
    o = jnp.concatenate(acc, axis=1).reshape(tb, d)

    blk = lambda x, i: x[i * c:(i + 1) * c, :]
    kv = [_mm(blk(v, i), blk(ke, i), _TN) for i in range(nb)]
    per_seq = nb // nseq
    states = [st_ref[s] for s in range(nseq)]
    starts = []
    for i in range(nb):
        s = i // per_seq
        starts.append(states[s])
        states[s] = states[s] * d_end[i] + kv[i]
    for s in range(nseq):
        st_ref[s] = states[s]
    o = o + jnp.concatenate([_mm(blk(qe, i), starts[i], _NT) for i in range(nb)], axis=0)

    o = o * lax.rsqrt(jnp.mean(o * o, -1, keepdims=True) + RMS_EPS)
    gh = gh_ref[...]
    o_ref[...] = (o * ng_ref[...] * (gh * jax.nn.sigmoid(gh))).astype(o_ref.dtype)

    @pl.when(t == nt - 1)
    def _():
        for s in range(nseq):
            so_ref[s] = states[s].T


def _seq_tiling(B, T, rows):
    if T >= rows:
        tb = _row_tile(T, rows)
        return 1, tb, T // tb
    nseq = _row_tile(B, max(1, rows // T))
    return nseq, nseq * T, 1


def _hgrn(z, lb, norm_g, s0, *, B, T, lay):
    nh = lay.d_hg // HG_HEAD_DIM
    nseq, tb, nt = _seq_tiling(B, T, RECURRENCE_ROWS)
    d = HG_HEAD_DIM
    zspec = lambda sec: pl.BlockSpec((tb, d), lambda b, h, t: (b * nt + t, sec * nh + h))
    pspec = pl.BlockSpec((1, d), lambda b, h, t: (0, h))
    sspec = pl.BlockSpec((nseq, None, d, d), lambda b, h, t: (b, h, 0, 0))
    return pl.pallas_call(
        functools.partial(_hgrn_body, tb=tb, nseq=nseq, nt=nt),
        grid=(B // nseq, nh, nt),
        in_specs=[zspec(0), zspec(1), zspec(2), zspec(3), pspec, pspec, sspec],
        out_specs=[pl.BlockSpec((tb, d), lambda b, h, t: (b * nt + t, h)), sspec],
        out_shape=[jax.ShapeDtypeStruct((B * T, lay.d_hg), BF16), jax.ShapeDtypeStruct(s0.shape, F32)],
        scratch_shapes=[pltpu.VMEM((nseq, d, d), F32)],
        compiler_params=_params("parallel", "parallel", "arbitrary"),
        name="hgrn",
    )(z, z, z, z, lb, norm_g, s0)


def _rwkv_chunks(r, lw, k, v, a, b, states, C):
    tb, W = r.shape
    nc = tb // C
    C2 = 2 * C
    chunks = range(nc)
    Lc = _block_cumsum(lw, C)
    Lc3 = Lc.reshape(nc, C, W)
    l_end = Lc3[:, C - 1:C, :]
    e_inv = jnp.exp(-Lc)
    e_end = jnp.exp(l_end - Lc3).reshape(tb, W)
    w_end = jnp.exp(l_end)
    in_a = lax.broadcasted_iota(jnp.int32, (1, W), 1) < (W // 2)

    def stacked(x):
        xa = jnp.where(in_a, x, 0.0)
        xb = x - xa
        return [jnp.concatenate([xa[c * C:(c + 1) * C], xb[c * C:(c + 1) * C]], axis=0) for c in chunks]

    Rs = stacked(r * jnp.exp(Lc))
    As = [x.astype(BF16) for x in stacked(a * jnp.exp(Lc - lw))]
    Bs = stacked(b * e_inv)
    Ks = stacked(k * e_inv)
    Bhs = [x.astype(BF16) for x in stacked(b * e_end)]
    Khs = [x.astype(BF16) for x in stacked(k * e_end)]
    Vs = [x.astype(BF16) for x in stacked(v)]

    row = lax.broadcasted_iota(jnp.int32, (C2, C2), 0)
    col = lax.broadcasted_iota(jnp.int32, (C2, C2), 1)
    strict = col < row
    incl = col <= row
    if C2 % LANES == 0:
        sc = [_mm(jnp.concatenate([As[c], Rs[c]], axis=0), jnp.concatenate([Bs[c], Ks[c]], axis=0), _NT)
              for c in chunks]
        s_ab = [x[:C2, :C2] for x in sc]
        s_ak = [x[:C2, C2:] for x in sc]
        s_rb = [x[C2:, :C2] for x in sc]
        s_rk = [x[C2:, C2:] for x in sc]
    else:
        s_ab = [_mm(As[c], Bs[c], _NT) for c in chunks]
        s_ak = [_mm(As[c], Ks[c], _NT) for c in chunks]
        s_rb = [_mm(Rs[c], Bs[c], _NT) for c in chunks]
        s_rk = [_mm(Rs[c], Ks[c], _NT) for c in chunks]
    a_ab = [jnp.where(strict, x, 0.0) for x in s_ab]
    a_ak = [jnp.where(strict, x, 0.0).astype(BF16) for x in s_ak]
    a_rb = [jnp.where(incl, x, 0.0).astype(BF16) for x in s_rb]
    a_rk = [jnp.where(incl, x, 0.0).astype(BF16) for x in s_rk]

    eye = jnp.where(row == col, 1.0, 0.0)
    tinv = [eye + x for x in a_ab]
    pw = [_mm(x, x) for x in a_ab]
    levels = int(math.log2(C)) - 1
    for lvl in range(levels):
        if lvl == levels - 1:
            tinv = [tinv[c] + _mm(pw[c], tinv[c]) for c in chunks]
        elif C2 % LANES == 0:
            both = [_mm(pw[c], jnp.concatenate([pw[c], tinv[c]], axis=1)) for c in chunks]
            pw = [x[:, :C2] for x in both]
            tinv = [tinv[c] + both[c][:, C2:] for c in chunks]
        else:
            tinv = [tinv[c] + _mm(pw[c], tinv[c]) for c in chunks]
            pw = [_mm(x, x) for x in pw]

    av = [_mm(a_ak[c], Vs[c]) for c in chunks]
    pq = [_mm(tinv[c], jnp.concatenate([As[c], av[c].astype(BF16)], axis=1)) for c in chunks]
    pq_b = [x.astype(BF16) for x in pq]
    bp = [_mm(Bhs[c], pq_b[c], _TN) for c in chunks]
    kv = [_mm(Khs[c], Vs[c], _TN) for c in chunks]
    rq = [_mm(a_rb[c], pq_b[c]) for c in chunks]
    ry = [Rs[c] + rq[c][:, :W] for c in chunks]
    yc = [rq[c][:, W:] + _mm(a_rk[c], Vs[c]) for c in chunks]

    rk = lax.broadcasted_iota(jnp.int32, (W, W), 0)
    ck = lax.broadcasted_iota(jnp.int32, (W, W), 1)
    diag = rk == ck
    states = list(states)
    per_seq = nc // len(states)
    starts = []
    for c in chunks:
        s = c // per_seq
        starts.append(states[s])
        M = jnp.where(diag, jnp.broadcast_to(w_end[c], (W, W)), 0.0) + bp[c][:, :W]
        states[s] = _mm(M, states[s]) + (bp[c][:, W:] + kv[c])
    ys = [_mm(ry[c], starts[c]) + yc[c] for c in chunks]
    ys = [x[:C] + x[C:] for x in ys]
    return (ys[0] if nc == 1 else jnp.concatenate(ys, axis=0)), states


def _token_shift(ref, carry, mu_ref, nseq):
    x = ref[...]
    tb, w = x.shape
    tseq = tb // nseq
    seq_start = lax.broadcasted_iota(jnp.int32, (tb, 1), 0) % tseq == 0
    before = jnp.broadcast_to(carry[...], (nseq, tseq, w)).reshape(tb, w)
    prev = jnp.where(seq_start, before, pltpu.roll(x, 1, 0))
    carry[...] = x.reshape(nseq, tseq, w)[:, tseq - 1:tseq, :]
    return x + (prev - x) * mu_ref[...]


def _rwkv_lora_body(lo_ref, mu_ref, s0_ref, w0_ref, w2_ref, a0_ref, a2_ref, g2_ref,
                    lw_ref, a_ref, g_ref, carry, *, nseq, dl_w, dl_a, dl_g):
    @pl.when(pl.program_id(1) == 0)
    def _():
        carry[...] = s0_ref[...]

    lo = _token_shift(lo_ref, carry, mu_ref, nseq)
    wd = lo[:, :dl_w]
    ad = lo[:, dl_w:dl_w + dl_a]
    gd = lo[:, dl_w + dl_a:dl_w + dl_a + dl_g]
    w_log = -jax.nn.softplus(-(w0_ref[...] + _mm_wide(jnp.tanh(wd), w2_ref[...]))) - 0.5
    lw_ref[...] = -jnp.exp(w_log)
    a_ref[...] = jax.nn.sigmoid(a0_ref[...] + _mm(ad, a2_ref[...]))
    g_ref[...] = _mm(jax.nn.sigmoid(gd), g2_ref[...]).astype(g_ref.dtype)


def _rwkv_lora(z, mu_z, shift_z, prm, *, B, T, lay):
    nseq, tb, nt = _seq_tiling(B, T, RECURRENCE_ROWS)
    lw_ = lay.lora_w
    c_lo = lay.off_lora // lw_
    d_rw = lay.d_rw
    dl_w, dl_a, dl_g = prm["w2"].shape[0], prm["a2"].shape[0], prm["g2"].shape[0]
    full = lambda a: pl.BlockSpec(a.shape, lambda b, t: (0, 0))
    out_spec = pl.BlockSpec((tb, d_rw), lambda b, t: (b * nt + t, 0))
    return pl.pallas_call(
        functools.partial(_rwkv_lora_body, nseq=nseq, dl_w=dl_w, dl_a=dl_a, dl_g=dl_g),
        grid=(B // nseq, nt),
        in_specs=[pl.BlockSpec((tb, lw_), lambda b, t: (b * nt + t, c_lo)),
                  pl.BlockSpec((1, lw_), lambda b, t: (0, c_lo)),
                  pl.BlockSpec((nseq, 1, lw_), lambda b, t: (b, 0, c_lo)),
                  full(prm["w0"]), full(prm["w2"]), full(prm["a0"]), full(prm["a2"]), full(prm["g2"])],
        out_specs=[out_spec, out_spec, out_spec],
        out_shape=[jax.ShapeDtypeStruct((B * T, d_rw), F32), jax.ShapeDtypeStruct((B * T, d_rw), F32),
                   jax.ShapeDtypeStruct((B * T, d_rw), BF16)],
        scratch_shapes=[pltpu.VMEM((nseq, 1, lw_), F32)],
        compiler_params=_params("parallel", "arbitrary"),
        name="rwkv_lora",
    )(z, mu_z, shift_z, prm["w0"], prm["w2"], prm["a0"], prm["a2"], prm["g2"])


def _rwkv_body(r_ref, k_ref, v_ref, lw_ref, a_ref, g_ref, mur_ref, muk_ref, muv_ref,
               sr_ref, sk_ref, sv_ref, kk_ref, ka_ref, rk_ref, lng_ref, lnb_ref, h0_ref,
               y_ref, ho_ref, h_sc, cr_sc, ck_sc, cv_sc, *, nseq, chunk, nt):
    t = pl.program_id(2)

    @pl.when(t == 0)
    def _():
        h_sc[...] = h0_ref[...]
        cr_sc[...] = sr_ref[...]
        ck_sc[...] = sk_ref[...]
        cv_sc[...] = sv_ref[...]

    rs = _token_shift(r_ref, cr_sc, mur_ref, nseq)
    ks = _token_shift(k_ref, ck_sc, muk_ref, nseq)
    vs = _token_shift(v_ref, cv_sc, muv_ref, nseq)
    lw = lw_ref[...]
    a_lr = a_ref[...]

    in_a = lax.broadcasted_iota(jnp.int32, (1, RW_PAIR), 1) < RW_HEAD_DIM

    def head_sum(x):
        total = jnp.sum(x, -1, keepdims=True)
        first = jnp.sum(jnp.where(in_a, x, 0.0), -1, keepdims=True)
        return jnp.where(in_a, first, total - first)

    kk = ks * kk_ref[...]
    kk = kk / jnp.maximum(jnp.sqrt(head_sum(kk * kk)), 1e-12)
    k2 = ks * (1.0 + (a_lr - 1.0) * ka_ref[...])

    y, states = _rwkv_chunks(rs, lw, k2, vs, -kk, kk * a_lr, [h_sc[s] for s in range(nseq)], chunk)
    for s in range(nseq):
        h_sc[s] = states[s]

    inv_n = 1.0 / RW_HEAD_DIM
    mu = head_sum(y) * inv_n
    dev = y - mu
    var = head_sum(dev * dev) * inv_n
    yn = dev * lax.rsqrt(var + RW_GN_EPS) * lng_ref[...] + lnb_ref[...]
    bonus = head_sum(rs * k2 * rk_ref[...]) * vs
    y_ref[...] = ((yn + bonus) * g_ref[...].astype(F32)).astype(y_ref.dtype)

    @pl.when(t == nt - 1)
    def _():
        for s in range(nseq):
            ho_ref[s] = states[s]


def _rwkv(z, lw, a_lr, g, mu_z, shift_z, prm, h0, *, B, T, lay):
    npair = lay.d_rw // RW_PAIR
    nseq, tb, nt = _seq_tiling(B, T, RECURRENCE_ROWS)
    w = RW_PAIR
    c_r, c_k, c_v = (lay.off_r // w, lay.off_r // w + npair, lay.off_r // w + 2 * npair)
    zs = lambda c0: pl.BlockSpec((tb, w), lambda b, p, t: (b * nt + t, c0 + p))
    ms = lambda c0: pl.BlockSpec((1, w), lambda b, p, t: (0, c0 + p))
    ss = lambda c0: pl.BlockSpec((nseq, 1, w), lambda b, p, t: (b, 0, c0 + p))
    ps = pl.BlockSpec((1, w), lambda b, p, t: (0, p))
    hs = pl.BlockSpec((nseq, None, w, w), lambda b, p, t: (b, p, 0, 0))
    return pl.pallas_call(
        functools.partial(_rwkv_body, nseq=nseq, chunk=min(RW_CHUNK, tb // nseq), nt=nt),
        grid=(B // nseq, npair, nt),
        in_specs=[
            zs(c_r), zs(c_k), zs(c_v), zs(0), zs(0), zs(0),
            ms(c_r), ms(c_k), ms(c_v),
            ss(c_r), ss(c_k), ss(c_v),
            ps, ps, ps, ps, ps, hs,
        ],
        out_specs=[pl.BlockSpec((tb, w), lambda b, p, t: (b * nt + t, p)), hs],
        out_shape=[jax.ShapeDtypeStruct((B * T, lay.d_rw), BF16), jax.ShapeDtypeStruct(h0.shape, F32)],
        scratch_shapes=[pltpu.VMEM((nseq, w, w), F32), pltpu.VMEM((nseq, 1, w), F32), pltpu.VMEM((nseq, 1, w), F32),
                        pltpu.VMEM((nseq, 1, w), F32)],
        compiler_params=_params("parallel", "parallel", "arbitrary"),
        name="rwkv",
    )(z, z, z, lw, a_lr, g, mu_z, mu_z, mu_z, shift_z, shift_z, shift_z,
      prm["k_k"], prm["k_a"], prm["r_k"], prm["ln_g"], prm["ln_b"], h0)


def _merge_up_body(o_ref, y_ref, ga_ref, gb_ref, wa_ref, wb_ref, m_ref):
    ua = jnp.dot(o_ref[...], wa_ref[...], preferred_element_type=F32)
    ub = jnp.dot(y_ref[...], wb_ref[...], preferred_element_type=F32)
    m = jax.nn.sigmoid(ga_ref[...]) * ua + jax.nn.sigmoid(gb_ref[...]) * ub
    m_ref[...] = m.astype(m_ref.dtype)


def _merge_up(o, y, zg, wa_b, wb_b, *, tm, tn, lay):
    M = o.shape[0]
    D = lay.d_model
    c_ga, c_gb = 0, D // tn
    return pl.pallas_call(
        _merge_up_body,
        grid=(M // tm, D // tn),
        in_specs=[
            pl.BlockSpec((tm, lay.d_hg), lambda i, j: (i, 0)),
            pl.BlockSpec((tm, lay.d_rw), lambda i, j: (i, 0)),
            pl.BlockSpec((tm, tn), lambda i, j: (i, c_ga + j)),
            pl.BlockSpec((tm, tn), lambda i, j: (i, c_gb + j)),
            pl.BlockSpec((lay.d_hg, tn), lambda i, j: (0, j)),
            pl.BlockSpec((lay.d_rw, tn), lambda i, j: (0, j)),
        ],
        out_specs=pl.BlockSpec((tm, tn), lambda i, j: (i, j)),
        out_shape=jax.ShapeDtypeStruct((M, D), BF16),
        compiler_params=_params("parallel", "arbitrary"),
        name="merge_up",
    )(o, y, zg, zg, wa_b, wb_b)


class _ZLayout:
    def __init__(self, d_model, dl_w, dl_a, dl_g):
        self.d_hg = d_model // 2
        self.d_rw = d_model // 2
        self.d_model = d_model
        self.n_lora = dl_w + dl_a + dl_g
        self.off_r = 4 * self.d_hg
        self.off_lora = self.off_r + 3 * self.d_rw
        self.off_gates = self.off_lora + self.n_lora
        self.lora_w = 1024
        assert self.n_lora <= self.lora_w and self.off_lora % self.lora_w == 0
        self.width = self.off_lora + self.lora_w
        self.n_rw_in = 3 * self.d_rw + self.n_lora

    def rw_in_to_z(self, a):
        pads = [(0, 0)] * (a.ndim - 1) + [(self.off_r, self.width - self.off_r - self.n_rw_in)]
        return jnp.pad(a, pads)

    def z_to_rw_in(self, zrow):
        return zrow[..., self.off_r:self.off_r + self.n_rw_in]


def _row_tile(m, pref):
    t = min(pref, m)
    while m % t:
        t //= 2
    return t


def kernel(x_prompt, x_sample, state_hgrn, state_rwkv, state_shift, ln1_g, ln1_b, ffn1_w_in, ffn1_w_down,
           ln2_g, ln2_b, w_in, hg_lb, hg_norm_g, hg_proj, rw_mu, rw_w0, rw_w2, rw_a0, rw_a2, rw_g2, rw_k_k,
           rw_k_a, rw_r_k, rw_ln_g, rw_ln_b, rw_proj, w_out, ln3_g, ln3_b, ffn2_w_in, ffn2_w_down):
    depth = ffn1_w_in.shape[0]
    assert depth == 1, "single-layer stack"
    D = x_prompt.shape[-1]
    alpha = (2 * depth) ** 0.25
    dl_w, dl_a, dl_g = rw_w2.shape[1], rw_a2.shape[1], rw_g2.shape[1]
    assert dl_w % LANES == 0 and dl_a % LANES == 0
    dl_gp = -(-dl_g // LANES) * LANES
    lay = _ZLayout(D, dl_w, dl_a, dl_g)
    nh = lay.d_hg // HG_HEAD_DIM
    nrw = lay.d_rw // RW_HEAD_DIM
    npair = nrw // 2
    l = 0

    row = lambda p: p[l].reshape(1, -1).astype(F32)
    f1_in, f1_dn = ffn1_w_in[l].astype(BF16), _column_panels(ffn1_w_down[l].astype(BF16), FFN_DOWN_COLS)
    f2_in, f2_dn = ffn2_w_in[l].astype(BF16), _column_panels(ffn2_w_down[l].astype(BF16), FFN_DOWN_COLS)
    w_in_b = w_in[l].astype(BF16)
    w_gates_b = w_in_b[:, lay.off_gates:]
    assert w_gates_b.shape[1] == 2 * D
    wa_b, wb_b = hg_proj[l].astype(BF16), rw_proj[l].astype(BF16)
    wo_b = _column_panels(w_out[l].astype(BF16), min(OUT_PROJ_COLS, D))
    lb = jnp.cumsum(jax.nn.softmax(hg_lb.astype(F32), axis=0), axis=0)[l].reshape(1, -1)
    mu_z = lay.rw_in_to_z(rw_mu[l].reshape(1, -1))
    prm = dict(w0=row(rw_w0), w2=rw_w2[l], a0=row(rw_a0), a2=rw_a2[l],
               g2=jnp.pad(rw_g2[l], ((0, dl_gp - dl_g), (0, 0))).astype(BF16),
               k_k=row(rw_k_k), k_a=row(rw_k_a), r_k=row(rw_r_k), ln_g=row(rw_ln_g), ln_b=row(rw_ln_b))

    def ffn(x, xb, w_in_b, w_dn_b, g, b, tm_dn, with_bf16):
        ff = w_dn_b.shape[1]
        tn_up = max(t for t in (LANES, 2 * LANES, 4 * LANES) if ff % t == 0)
        act = _gated_up(xb, w_in_b, tm=_row_tile(x.shape[0], 512 * 1024 // tn_up), tn=tn_up)
        return _down_ln(act, w_dn_b, x, g, b, alpha=alpha, scale=0.5, tm=tm_dn, with_bf16=with_bf16)

    def trunk(x, hg_s0, rw_s0, shift0):
        B, T, _ = x.shape
        M = B * T
        x0 = x.reshape(M, D)
        tm_up = _row_tile(M, 1024)
        tm_dn = _row_tile(M, 512)
        x1, x1b = ffn(x0, x0.astype(BF16), f1_in, f1_dn, row(ln1_g), row(ln1_b), tm_dn, True)
        z = _zproj(x1b, w_in_b, n_cols=lay.width, tm=tm_up, tn=1024)
        zg = _zproj(x1b, w_gates_b, n_cols=2 * D, tm=tm_up, tn=1024)
        o, hg_s = _hgrn(z, lb, row(hg_norm_g), hg_s0, B=B, T=T, lay=lay)
        st = jnp.swapaxes(rw_s0, -1, -2).reshape(B, npair, 2, RW_HEAD_DIM, RW_HEAD_DIM)
        zero = jnp.zeros_like(st[:, :, 0])
        h0 = jnp.concatenate([jnp.concatenate([st[:, :, 0], zero], -1),
                              jnp.concatenate([zero, st[:, :, 1]], -1)], -2)
        shift_z = lay.rw_in_to_z(shift0)
        lw, a_lr, g = _rwkv_lora(z, mu_z, shift_z, prm, B=B, T=T, lay=lay)
        y, h_out = _rwkv(z, lw, a_lr, g, mu_z, shift_z, prm, h0, B=B, T=T, lay=lay)
        hd = RW_HEAD_DIM
        rw_s = jnp.stack([h_out[:, :, :hd, :hd], h_out[:, :, hd:, hd:]], axis=2)
        rw_s = jnp.swapaxes(rw_s.reshape(B, nrw, hd, hd), -1, -2)
        m = _merge_up(o, y, zg, wa_b, wb_b, tm=tm_up, tn=512, lay=lay)
        x2, x2b = _down_ln(m, wo_b, x1, row(ln2_g), row(ln2_b), alpha=alpha, scale=1.0,
                           tm=tm_dn, with_bf16=True)
        (x3,) = ffn(x2, x2b, f2_in, f2_dn, row(ln3_g), row(ln3_b), tm_dn, False)
        shift = lay.z_to_rw_in(z.reshape(B, T, -1)[:, -1:, :])
        return x3.reshape(B, T, D), hg_s[None], rw_s[None], shift[None]

    Bp = x_prompt.shape[0]
    hg0 = jnp.zeros((Bp, nh, HG_HEAD_DIM, HG_HEAD_DIM), F32)
    rw0 = jnp.zeros((Bp, nrw, RW_HEAD_DIM, RW_HEAD_DIM), F32)
    sh0 = jnp.zeros((Bp, 1, rw_mu.shape[-1]), F32)
    y_p, hg_p, rw_p, sh_p = trunk(x_prompt, hg0, rw0, sh0)
    y_s, hg_s, rw_s, sh_s = trunk(x_sample, state_hgrn[l].astype(F32), state_rwkv[l].astype(F32),
                                  state_shift[l].astype(F32))
    return (y_p, y_s, hg_p, rw_p, sh_p, hg_s, rw_s, sh_s)
```

```python
import functools
import math

import jax
import jax.numpy as jnp
from jax import lax
from jax.experimental import pallas as pl
from jax.experimental.pallas import tpu as pltpu

F32 = jnp.float32
BF16 = jnp.bfloat16

LANES = 128
HG_HEAD_DIM = 128
RW_HEAD_DIM = 64
RW_PAIR = 2 * RW_HEAD_DIM
HG_BLOCK = 16
RW_CHUNK = 64
RECURRENCE_ROWS = 512
RW_ROWS = 128
RW_CHUNKS_PER_STEP = 32
LN_EPS = 1e-5
RMS_EPS = 1e-6
RW_GN_EPS = 64e-5
VMEM_LIMIT = 56 * 1024 * 1024
DOWN_VMEM_LIMIT = 60 * 1024 * 1024
PANEL_ROWS = 32
FFN_DOWN_COLS = 256
OUT_PROJ_COLS = 1024

_NN = (((1,), (0,)), ((), ()))
_NT = (((1,), (1,)), ((), ()))
_TN = (((0,), (0,)), ((), ()))


def _mm(a, b, dims=_NN):
    return lax.dot_general(a.astype(BF16), b.astype(BF16), dims, preferred_element_type=F32)


def _split(x, n):
    parts = []
    for _ in range(n - 1):
        p = x.astype(BF16)
        parts.append(p)
        x = x - p.astype(F32)
    parts.append(x.astype(BF16))
    return parts


def _mm_wide(a, b):
    ah, am = _split(a, 2)
    bh, bm = _split(b, 2)
    dot = lambda p, q: jnp.dot(p, q, preferred_element_type=F32)
    return dot(ah, bh) + (dot(ah, bm) + dot(am, bh))


def _mask_mm(mask, x, terms=3):
    mb = mask.astype(BF16)
    return sum(jnp.dot(mb, p, preferred_element_type=F32) for p in _split(x, terms))


def _layer_norm(y, g, b):
    mu = jnp.mean(y, -1, keepdims=True)
    d = y - mu
    var = jnp.mean(d * d, -1, keepdims=True)
    return d * lax.rsqrt(var + LN_EPS) * g + b


def _params(*sem):
    return pltpu.CompilerParams(dimension_semantics=sem, vmem_limit_bytes=VMEM_LIMIT)


def _gated_up_body(x_ref, wg_ref, wu_ref, o_ref):
    x = x_ref[...]
    h = jnp.dot(x, wg_ref[...], preferred_element_type=F32)
    u = jnp.dot(x, wu_ref[...], preferred_element_type=F32)
    o_ref[...] = (h * jax.nn.sigmoid(h) * u).astype(o_ref.dtype)


def _gated_up(xb, w_in_b, *, tm, tn):
    M, D = xb.shape
    F = w_in_b.shape[1] // 2
    nj = F // tn
    return pl.pallas_call(
        _gated_up_body,
        grid=(M // tm, nj),
        in_specs=[pl.BlockSpec((tm, D), lambda i, j: (i, 0)),
                  pl.BlockSpec((D, tn), lambda i, j: (0, j)),
                  pl.BlockSpec((D, tn), lambda i, j: (0, j + nj))],
        out_specs=pl.BlockSpec((tm, tn), lambda i, j: (i, j)),
        out_shape=jax.ShapeDtypeStruct((M, F), BF16),
        compiler_params=_params("parallel", "arbitrary"),
        name="gated_up",
    )(xb, w_in_b, w_in_b)


def _down_ln_body(a_ref, w_ref, x_ref, g_ref, b_ref, o_ref, *maybe_ob_ref, alpha, scale, nj, tn):
    j = pl.program_id(1)
    c0 = pl.multiple_of(j * tn, tn)
    o_ref[:, pl.ds(c0, tn)] = jnp.dot(a_ref[...], w_ref[...], preferred_element_type=F32)

    @pl.when(j == nj - 1)
    def _():
        g = g_ref[...]
        b = b_ref[...]
        nrow = min(PANEL_ROWS, o_ref.shape[0])

        def panel(i, carry):
            r0 = pl.multiple_of(i * nrow, nrow)
            rows = pl.ds(r0, nrow)
            y = _layer_norm(alpha * x_ref[rows, :] + scale * o_ref[rows, :], g, b)
            o_ref[rows, :] = y
            for ob_ref in maybe_ob_ref:
                ob_ref[rows, :] = y.astype(ob_ref.dtype)
            return carry

        lax.fori_loop(0, o_ref.shape[0] // nrow, panel, 0)


def _down_ln(act, w_b, x, ln_g, ln_b, *, alpha, scale, tm, tn, with_bf16):
    M, D = x.shape
    K = act.shape[1]
    nj = D // tn
    once = dict(pipeline_mode=pl.Buffered(1))
    row_spec = pl.BlockSpec((tm, D), lambda i, j: (i, 0))
    out_shape = [jax.ShapeDtypeStruct((M, D), F32)] + ([jax.ShapeDtypeStruct((M, D), BF16)] if with_bf16 else [])
    return pl.pallas_call(
        functools.partial(_down_ln_body, alpha=alpha, scale=scale, nj=nj, tn=tn),
        grid=(M // tm, nj),
        in_specs=[pl.BlockSpec((tm, K), lambda i, j: (i, 0), **once),
                  pl.BlockSpec((K, tn), lambda i, j: (0, j)),
                  pl.BlockSpec((tm, D), lambda i, j: (i, 0), **once),
                  pl.BlockSpec((1, D), lambda i, j: (0, 0)),
                  pl.BlockSpec((1, D), lambda i, j: (0, 0))],
        out_specs=[row_spec] * len(out_shape),
        out_shape=out_shape,
        compiler_params=pltpu.CompilerParams(dimension_semantics=("parallel", "arbitrary"),
                                             vmem_limit_bytes=DOWN_VMEM_LIMIT),
        name="down_ln",
    )(act, w_b, x, ln_g, ln_b)


def _zproj_body(x_ref, w_ref, o_ref):
    o_ref[...] = jnp.dot(x_ref[...], w_ref[...], preferred_element_type=F32)


def _zproj(xb, w_b, *, n_cols, tm, tn):
    M, D = xb.shape
    N = n_cols
    assert N % tn == 0 and N <= w_b.shape[1]
    return pl.pallas_call(
        _zproj_body,
        grid=(M // tm, N // tn),
        in_specs=[pl.BlockSpec((tm, D), lambda i, j: (i, 0)), pl.BlockSpec((D, tn), lambda i, j: (0, j))],
        out_specs=pl.BlockSpec((tm, tn), lambda i, j: (i, j)),
        out_shape=jax.ShapeDtypeStruct((M, N), F32),
        compiler_params=_params("parallel", "arbitrary"),
        name="zproj",
    )(xb, w_b)


def _block_cumsum(x, c):
    n, w = x.shape
    nb = n // c
    ri = lax.broadcasted_iota(jnp.int32, (c, c), 0)
    ci = lax.broadcasted_iota(jnp.int32, (c, c), 1)
    wide = x if nb == 1 else jnp.concatenate([x[i * c:(i + 1) * c, :] for i in range(nb)], axis=1)
    out = _mask_mm(jnp.where(ci <= ri, 1.0, 0.0), wide)
    return out if nb == 1 else jnp.concatenate([out[:, i * w:(i + 1) * w] for i in range(nb)], axis=0)


def _hgrn_body(q_ref, f_ref, v_ref, gh_ref, lb_ref, ng_ref, s0_ref, o_ref, so_ref, st_ref, *, tb, nseq, nt):
    t = pl.program_id(2)
    c = HG_BLOCK
    nb = tb // c
    d = HG_HEAD_DIM

    @pl.when(t == 0)
    def _():
        for s in range(nseq):
            st_ref[s] = s0_ref[s].T

    lb = lb_ref[...]
    q = q_ref[...]
    fp = f_ref[...]
    v = v_ref[...]
    logf = jnp.log(lb + (1.0 - lb) * jax.nn.sigmoid(fp))
    kf = (1.0 - lb) * jax.nn.sigmoid(-fp)
    L = _block_cumsum(logf, c)
    L3, q3, kf3, v3 = (x.reshape(nb, c, d) for x in (L, q, kf, v))
    l_end = L3[:, c - 1:c, :]
    qe = q * jnp.exp(L)
    ke = (kf3 * jnp.exp(l_end - L3)).reshape(tb, d)
    d_end = jnp.exp(l_end)

    h = c // 2
    rows = lax.broadcasted_iota(jnp.int32, (1, h, 1), 1)
    halves = [(L3[:, :h, :], q3[:, :h, :]), (L3[:, h:, :], q3[:, h:, :])]
    acc = [jnp.zeros((nb, h, d), F32), jnp.zeros((nb, h, d), F32)]
    for s in range(c):
        l_s, k_s, v_s = L3[:, s:s + 1, :], kf3[:, s:s + 1, :], v3[:, s:s + 1, :]
        for half, (l_t, q_t) in enumerate(halves):
            first = s - half * h
            if first >= h:
                continue
            diff = l_t - l_s
            if first > 0:
                diff = jnp.where(rows >= first, diff, -jnp.inf)
            sc = jnp.sum(jnp.exp(diff) * q_t * k_s, -1, keepdims=True)
            acc[half] = acc[half] + sc * v_s
    o = jnp.concatenate(acc, axis=1).reshape(tb, d)

    blk = lambda x, i: x[i * c:(i + 1) * c, :]
    kv = [_mm(blk(v, i), blk(ke, i), _TN) for i in range(nb)]
    per_seq = nb // nseq
    states = [st_ref[s] for s in range(nseq)]
    starts = []
    for i in range(nb):
        s = i // per_seq
        starts.append(states[s])
        states[s] = states[s] * d_end[i] + kv[i]
    for s in range(nseq):
        st_ref[s] = states[s]
    o = o + jnp.concatenate([_mm(blk(qe, i), starts[i], _NT) for i in range(nb)], axis=0)

    o = o * lax.rsqrt(jnp.mean(o * o, -1, keepdims=True) + RMS_EPS)
    gh = gh_ref[...]
    o_ref[...] = (o * ng_ref[...] * (gh * jax.nn.sigmoid(gh))).astype(o_ref.dtype)

    @pl.when(t == nt - 1)
    def _():
        for s in range(nseq):
            so_ref[s] = states[s].T


def _seq_tiling(B, T, rows):
    if T >= rows:
        tb = _row_tile(T, rows)
        return 1, tb, T // tb
    nseq = _row_tile(B, max(1, rows // T))
    return nseq, nseq * T, 1


def _hgrn(z, lb, norm_g, s0, *, B, T, lay):
    nh = lay.d_hg // HG_HEAD_DIM
    nseq, tb, nt = _seq_tiling(B, T, RECURRENCE_ROWS)
    d = HG_HEAD_DIM
    zspec = lambda sec: pl.BlockSpec((tb, d), lambda b, h, t: (b * nt + t, sec * nh + h))
    pspec = pl.BlockSpec((1, d), lambda b, h, t: (0, h))
    sspec = pl.BlockSpec((nseq, None, d, d), lambda b, h, t: (b, h, 0, 0))
    return pl.pallas_call(
        functools.partial(_hgrn_body, tb=tb, nseq=nseq, nt=nt),
        grid=(B // nseq, nh, nt),
        in_specs=[zspec(0), zspec(1), zspec(2), zspec(3), pspec, pspec, sspec],
        out_specs=[pl.BlockSpec((tb, d), lambda b, h, t: (b * nt + t, h)), sspec],
        out_shape=[jax.ShapeDtypeStruct((B * T, lay.d_hg), BF16), jax.ShapeDtypeStruct(s0.shape, F32)],
        scratch_shapes=[pltpu.VMEM((nseq, d, d), F32)],
        compiler_params=_params("parallel", "parallel", "arbitrary"),
        name="hgrn",
    )(z, z, z, z, lb, norm_g, s0)


def _rwkv_chunks(r, lw, k, v, a, b, states, C):
    tb, W = r.shape
    nc = tb // C
    C2 = 2 * C
    chunks = range(nc)
    Lc = _block_cumsum(lw, C)
    Lc3 = Lc.reshape(nc, C, W)
    l_end = Lc3[:, C - 1:C, :]
    e_inv = jnp.exp(-Lc)
    e_end = jnp.exp(l_end - Lc3).reshape(tb, W)
    w_end = jnp.exp(l_end)
    in_a = lax.broadcasted_iota(jnp.int32, (1, W), 1) < (W // 2)

    def stacked(x):
        xa = jnp.where(in_a, x, 0.0)
        xb = x - xa
        return [jnp.concatenate([xa[c * C:(c + 1) * C], xb[c * C:(c + 1) * C]], axis=0) for c in chunks]

    Rs = stacked(r * jnp.exp(Lc))
    As = [x.astype(BF16) for x in stacked(a * jnp.exp(Lc - lw))]
    Bs = stacked(b * e_inv)
    Ks = stacked(k * e_inv)
    Bhs = [x.astype(BF16) for x in stacked(b * e_end)]
    Khs = [x.astype(BF16) for x in stacked(k * e_end)]
    Vs = [x.astype(BF16) for x in stacked(v)]

    row = lax.broadcasted_iota(jnp.int32, (C2, C2), 0)
    col = lax.broadcasted_iota(jnp.int32, (C2, C2), 1)
    strict = col < row
    incl = col <= row
    if C2 % LANES == 0:
        sc = [_mm(jnp.concatenate([As[c], Rs[c]], axis=0), jnp.concatenate([Bs[c], Ks[c]], axis=0), _NT)
              for c in chunks]
        s_ab = [x[:C2, :C2] for x in sc]
        s_ak = [x[:C2, C2:] for x in sc]
        s_rb = [x[C2:, :C2] for x in sc]
        s_rk = [x[C2:, C2:] for x in sc]
    else:
        s_ab = [_mm(As[c], Bs[c], _NT) for c in chunks]
        s_ak = [_mm(As[c], Ks[c], _NT) for c in chunks]
        s_rb = [_mm(Rs[c], Bs[c], _NT) for c in chunks]
        s_rk = [_mm(Rs[c], Ks[c], _NT) for c in chunks]
    a_ab = [jnp.where(strict, x, 0.0) for x in s_ab]
    a_ak = [jnp.where(strict, x, 0.0).astype(BF16) for x in s_ak]
    a_rb = [jnp.where(incl, x, 0.0).astype(BF16) for x in s_rb]
    a_rk = [jnp.where(incl, x, 0.0).astype(BF16) for x in s_rk]

    eye = jnp.where(row == col, 1.0, 0.0)
    tinv = [eye + x for x in a_ab]
    pw = [_mm(x, x) for x in a_ab]
    levels = int(math.log2(C)) - 1
    for lvl in range(levels):
        if lvl == levels - 1:
            tinv = [tinv[c] + _mm(pw[c], tinv[c]) for c in chunks]
        elif C2 % LANES == 0:
            both = [_mm(pw[c], jnp.concatenate([pw[c], tinv[c]], axis=1)) for c in chunks]
            pw = [x[:, :C2] for x in both]
            tinv = [tinv[c] + both[c][:, C2:] for c in chunks]
        else:
            tinv = [tinv[c] + _mm(pw[c], tinv[c]) for c in chunks]
            pw = [_mm(x, x) for x in pw]

    av = [_mm(a_ak[c], Vs[c]) for c in chunks]
    pq = [_mm(tinv[c], jnp.concatenate([As[c], av[c].astype(BF16)], axis=1)) for c in chunks]
    pq_b = [x.astype(BF16) for x in pq]
    bp = [_mm(Bhs[c], pq_b[c], _TN) for c in chunks]
    kv = [_mm(Khs[c], Vs[c], _TN) for c in chunks]
    rq = [_mm(a_rb[c], pq_b[c]) for c in chunks]
    ry = [Rs[c] + rq[c][:, :W] for c in chunks]
    yc = [rq[c][:, W:] + _mm(a_rk[c], Vs[c]) for c in chunks]

    rk = lax.broadcasted_iota(jnp.int32, (W, W), 0)
    ck = lax.broadcasted_iota(jnp.int32, (W, W), 1)
    diag = rk == ck
    states = list(states)
    per_seq = nc // len(states)
    starts = []
    for c in chunks:
        s = c // per_seq
        starts.append(states[s])
        M = jnp.where(diag, jnp.broadcast_to(w_end[c], (W, W)), 0.0) + bp[c][:, :W]
        states[s] = _mm(M, states[s]) + (bp[c][:, W:] + kv[c])
    ys = [_mm(ry[c], starts[c]) + yc[c] for c in chunks]
    ys = [x[:C] + x[C:] for x in ys]
    return (ys[0] if nc == 1 else jnp.concatenate(ys, axis=0)), states


def _token_shift(ref, carry, mu_ref, nseq):
    x = ref[...]
    tb, w = x.shape
    tseq = tb // nseq
    seq_start = lax.broadcasted_iota(jnp.int32, (tb, 1), 0) % tseq == 0
    before = jnp.broadcast_to(carry[...], (nseq, tseq, w)).reshape(tb, w)
    prev = jnp.where(seq_start, before, pltpu.roll(x, 1, 0))
    carry[...] = x.reshape(nseq, tseq, w)[:, tseq - 1:tseq, :]
    return x + (prev - x) * mu_ref[...]


def _rwkv_lora_body(lo_ref, mu_ref, s0_ref, w0_ref, w2_ref, a0_ref, a2_ref, g2_ref,
                    lw_ref, a_ref, g_ref, carry, *, nseq, dl_w, dl_a, dl_g):
    @pl.when(pl.program_id(1) == 0)
    def _():
        carry[...] = s0_ref[...]

    lo = _token_shift(lo_ref, carry, mu_ref, nseq)
    wd = lo[:, :dl_w]
    ad = lo[:, dl_w:dl_w + dl_a]
    gd = lo[:, dl_w + dl_a:dl_w + dl_a + dl_g]
    w_log = -jax.nn.softplus(-(w0_ref[...] + _mm_wide(jnp.tanh(wd), w2_ref[...]))) - 0.5
    lw_ref[...] = -jnp.exp(w_log)
    a_ref[...] = jax.nn.sigmoid(a0_ref[...] + _mm(ad, a2_ref[...]))
    g_ref[...] = _mm(jax.nn.sigmoid(gd), g2_ref[...]).astype(g_ref.dtype)


def _rwkv_lora(z, mu_z, shift_z, prm, *, B, T, lay):
    nseq, tb, nt = _seq_tiling(B, T, RECURRENCE_ROWS)
    lw_ = lay.lora_w
    c_lo = lay.off_lora // lw_
    d_rw = lay.d_rw
    dl_w, dl_a, dl_g = prm["w2"].shape[0], prm["a2"].shape[0], prm["g2"].shape[0]
    full = lambda a: pl.BlockSpec(a.shape, lambda b, t: (0, 0))
    out_spec = pl.BlockSpec((tb, d_rw), lambda b, t: (b * nt + t, 0))
    return pl.pallas_call(
        functools.partial(_rwkv_lora_body, nseq=nseq, dl_w=dl_w, dl_a=dl_a, dl_g=dl_g),
        grid=(B // nseq, nt),
        in_specs=[pl.BlockSpec((tb, lw_), lambda b, t: (b * nt + t, c_lo)),
                  pl.BlockSpec((1, lw_), lambda b, t: (0, c_lo)),
                  pl.BlockSpec((nseq, 1, lw_), lambda b, t: (b, 0, c_lo)),
                  full(prm["w0"]), full(prm["w2"]), full(prm["a0"]), full(prm["a2"]), full(prm["g2"])],
        out_specs=[out_spec, out_spec, out_spec],
        out_shape=[jax.ShapeDtypeStruct((B * T, d_rw), F32), jax.ShapeDtypeStruct((B * T, d_rw), F32),
                   jax.ShapeDtypeStruct((B * T, d_rw), BF16)],
        scratch_shapes=[pltpu.VMEM((nseq, 1, lw_), F32)],
        compiler_params=_params("parallel", "arbitrary"),
        name="rwkv_lora",
    )(z, mu_z, shift_z, prm["w0"], prm["w2"], prm["a0"], prm["a2"], prm["g2"])


def _rwkv_body(r_ref, k_ref, v_ref, lw_ref, a_ref, g_ref, mur_ref, muk_ref, muv_ref,
               sr_ref, sk_ref, sv_ref, kk_ref, ka_ref, rk_ref, lng_ref, lnb_ref, h0_ref,
               y_ref, ho_ref, h_sc, cr_sc, ck_sc, cv_sc, *, nseq, npg, chunk, nt):
    t = pl.program_id(2)
    w = RW_PAIR

    @pl.when(t == 0)
    def _():
        h_sc[...] = h0_ref[...]
        cr_sc[...] = sr_ref[...]
        ck_sc[...] = sk_ref[...]
        cv_sc[...] = sv_ref[...]

    rs = _token_shift(r_ref, cr_sc, mur_ref, nseq)
    ks = _token_shift(k_ref, ck_sc, muk_ref, nseq)
    vs = _token_shift(v_ref, cv_sc, muv_ref, nseq)
    a_lr = a_ref[...]
    tb = rs.shape[0]
    to_rows = lambda x: x if npg == 1 else jnp.concatenate([x[:, p * w:(p + 1) * w] for p in range(npg)], axis=0)
    to_lanes = lambda x: x if npg == 1 else jnp.concatenate([x[p * tb:(p + 1) * tb] for p in range(npg)], axis=1)

    in_a = lax.broadcasted_iota(jnp.int32, (1, w), 1) < RW_HEAD_DIM

    def head_sum(x):
        total = jnp.sum(x, -1, keepdims=True)
        first = jnp.sum(jnp.where(in_a, x, 0.0), -1, keepdims=True)
        return jnp.where(in_a, first, total - first)

    kk = to_rows(ks * kk_ref[...])
    kk = kk / jnp.maximum(jnp.sqrt(head_sum(kk * kk)), 1e-12)
    k2 = ks * (1.0 + (a_lr - 1.0) * ka_ref[...])
    a_rows = to_rows(a_lr)
    v_rows = to_rows(vs)

    states = [h_sc[s, p] for p in range(npg) for s in range(nseq)]
    y, states = _rwkv_chunks(to_rows(rs), to_rows(lw_ref[...]), to_rows(k2), v_rows, -kk, kk * a_rows, states, chunk)
    for p in range(npg):
        for s in range(nseq):
            h_sc[s, p] = states[p * nseq + s]

    inv_n = 1.0 / RW_HEAD_DIM
    mu = head_sum(y) * inv_n
    dev = y - mu
    var = head_sum(dev * dev) * inv_n
    yn = to_lanes(dev * lax.rsqrt(var + RW_GN_EPS)) * lng_ref[...] + lnb_ref[...]
    bonus = to_lanes(head_sum(to_rows(rs * k2 * rk_ref[...])) * v_rows)
    y_ref[...] = ((yn + bonus) * g_ref[...].astype(F32)).astype(y_ref.dtype)

    @pl.when(t == nt - 1)
    def _():
        ho_ref[...] = h_sc[...]


def _rwkv(z, lw, a_lr, g, mu_z, shift_z, prm, h0, *, B, T, lay):
    npair = lay.d_rw // RW_PAIR
    nseq, tb, nt = _seq_tiling(B, T, RW_ROWS)
    chunk = min(RW_CHUNK, tb // nseq)
    npg = math.gcd(npair, max(1, RW_CHUNKS_PER_STEP // (tb // chunk)))
    assert npair % npg == 0 and lay.off_r % (npg * RW_PAIR) == 0
    w = npg * RW_PAIR
    c_r, c_k, c_v = (lay.off_r // w, (lay.off_r + lay.d_rw) // w, (lay.off_r + 2 * lay.d_rw) // w)
    zs = lambda c0: pl.BlockSpec((tb, w), lambda b, p, t: (b * nt + t, c0 + p))
    ms = lambda c0: pl.BlockSpec((1, w), lambda b, p, t: (0, c0 + p))
    ss = lambda c0: pl.BlockSpec((nseq, 1, w), lambda b, p, t: (b, 0, c0 + p))
    ps = pl.BlockSpec((1, w), lambda b, p, t: (0, p))
    hs = pl.BlockSpec((nseq, npg, RW_PAIR, RW_PAIR), lambda b, p, t: (b, p, 0, 0))
    return pl.pallas_call(
        functools.partial(_rwkv_body, nseq=nseq, npg=npg, chunk=chunk, nt=nt),
        grid=(B // nseq, npair // npg, nt),
        in_specs=[
            zs(c_r), zs(c_k), zs(c_v), zs(0), zs(0), zs(0),
            ms(c_r), ms(c_k), ms(c_v),
            ss(c_r), ss(c_k), ss(c_v),
            ps, ps, ps, ps, ps, hs,
        ],
        out_specs=[pl.BlockSpec((tb, w), lambda b, p, t: (b * nt + t, p)), hs],
        out_shape=[jax.ShapeDtypeStruct((B * T, lay.d_rw), BF16), jax.ShapeDtypeStruct(h0.shape, F32)],
        scratch_shapes=[pltpu.VMEM((nseq, npg, RW_PAIR, RW_PAIR), F32), pltpu.VMEM((nseq, 1, w), F32),
                        pltpu.VMEM((nseq, 1, w), F32), pltpu.VMEM((nseq, 1, w), F32)],
        compiler_params=_params("parallel", "parallel", "arbitrary"),
        name="rwkv",
    )(z, z, z, lw, a_lr, g, mu_z, mu_z, mu_z, shift_z, shift_z, shift_z,
      prm["k_k"], prm["k_a"], prm["r_k"], prm["ln_g"], prm["ln_b"], h0)


def _merge_up_body(o_ref, y_ref, ga_ref, gb_ref, wa_ref, wb_ref, m_ref):
    ua = jnp.dot(o_ref[...], wa_ref[...], preferred_element_type=F32)
    ub = jnp.dot(y_ref[...], wb_ref[...], preferred_element_type=F32)
    m = jax.nn.sigmoid(ga_ref[...]) * ua + jax.nn.sigmoid(gb_ref[...]) * ub
    m_ref[...] = m.astype(m_ref.dtype)


def _merge_up(o, y, zg, wa_b, wb_b, *, tm, tn, lay):
    M = o.shape[0]
    D = lay.d_model
    c_ga, c_gb = 0, D // tn
    return pl.pallas_call(
        _merge_up_body,
        grid=(M // tm, D // tn),
        in_specs=[
            pl.BlockSpec((tm, lay.d_hg), lambda i, j: (i, 0)),
            pl.BlockSpec((tm, lay.d_rw), lambda i, j: (i, 0)),
            pl.BlockSpec((tm, tn), lambda i, j: (i, c_ga + j)),
            pl.BlockSpec((tm, tn), lambda i, j: (i, c_gb + j)),
            pl.BlockSpec((lay.d_hg, tn), lambda i, j: (0, j)),
            pl.BlockSpec((lay.d_rw, tn), lambda i, j: (0, j)),
        ],
        out_specs=pl.BlockSpec((tm, tn), lambda i, j: (i, j)),
        out_shape=jax.ShapeDtypeStruct((M, D), BF16),
        compiler_params=_params("parallel", "arbitrary"),
        name="merge_up",
    )(o, y, zg, zg, wa_b, wb_b)


class _ZLayout:
    def __init__(self, d_model, dl_w, dl_a, dl_g):
        self.d_hg = d_model // 2
        self.d_rw = d_model // 2
        self.d_model = d_model
        self.n_lora = dl_w + dl_a + dl_g
        self.off_r = 4 * self.d_hg
        self.off_lora = self.off_r + 3 * self.d_rw
        self.off_gates = self.off_lora + self.n_lora
        self.lora_w = 1024
        assert self.n_lora <= self.lora_w and self.off_lora % self.lora_w == 0
        self.width = self.off_lora + self.lora_w
        self.n_rw_in = 3 * self.d_rw + self.n_lora

    def rw_in_to_z(self, a):
        pads = [(0, 0)] * (a.ndim - 1) + [(self.off_r, self.width - self.off_r - self.n_rw_in)]
        return jnp.pad(a, pads)

    def z_to_rw_in(self, zrow):
        return zrow[..., self.off_r:self.off_r + self.n_rw_in]


def _row_tile(m, pref):
    t = min(pref, m)
    while m % t:
        t //= 2
    return t


def kernel(x_prompt, x_sample, state_hgrn, state_rwkv, state_shift, ln1_g, ln1_b, ffn1_w_in, ffn1_w_down,
           ln2_g, ln2_b, w_in, hg_lb, hg_norm_g, hg_proj, rw_mu, rw_w0, rw_w2, rw_a0, rw_a2, rw_g2, rw_k_k,
           rw_k_a, rw_r_k, rw_ln_g, rw_ln_b, rw_proj, w_out, ln3_g, ln3_b, ffn2_w_in, ffn2_w_down):
    depth = ffn1_w_in.shape[0]
    assert depth == 1, "single-layer stack"
    D = x_prompt.shape[-1]
    alpha = (2 * depth) ** 0.25
    dl_w, dl_a, dl_g = rw_w2.shape[1], rw_a2.shape[1], rw_g2.shape[1]
    assert dl_w % LANES == 0 and dl_a % LANES == 0
    dl_gp = -(-dl_g // LANES) * LANES
    lay = _ZLayout(D, dl_w, dl_a, dl_g)
    nh = lay.d_hg // HG_HEAD_DIM
    nrw = lay.d_rw // RW_HEAD_DIM
    npair = nrw // 2
    l = 0

    row = lambda p: p[l].reshape(1, -1).astype(F32)
    f1_in, f1_dn = ffn1_w_in[l].astype(BF16), ffn1_w_down[l].astype(BF16)
    f2_in, f2_dn = ffn2_w_in[l].astype(BF16), ffn2_w_down[l].astype(BF16)
    w_in_b = w_in[l].astype(BF16)
    w_gates_b = w_in_b[:, lay.off_gates:]
    assert w_gates_b.shape[1] == 2 * D
    wa_b, wb_b, wo_b = hg_proj[l].astype(BF16), rw_proj[l].astype(BF16), w_out[l].astype(BF16)
    lb = jnp.cumsum(jax.nn.softmax(hg_lb.astype(F32), axis=0), axis=0)[l].reshape(1, -1)
    mu_z = lay.rw_in_to_z(rw_mu[l].reshape(1, -1))
    prm = dict(w0=row(rw_w0), w2=rw_w2[l], a0=row(rw_a0), a2=rw_a2[l],
               g2=jnp.pad(rw_g2[l], ((0, dl_gp - dl_g), (0, 0))).astype(BF16),
               k_k=row(rw_k_k), k_a=row(rw_k_a), r_k=row(rw_r_k), ln_g=row(rw_ln_g), ln_b=row(rw_ln_b))

    def ffn(x, xb, w_in_b, w_dn_b, g, b, tm_dn, with_bf16):
        ff = w_dn_b.shape[0]
        tn_up = max(t for t in (LANES, 2 * LANES, 4 * LANES) if ff % t == 0)
        act = _gated_up(xb, w_in_b, tm=_row_tile(x.shape[0], 512 * 1024 // tn_up), tn=tn_up)
        return _down_ln(act, w_dn_b, x, g, b, alpha=alpha, scale=0.5, tm=tm_dn, tn=FFN_DOWN_COLS,
                        with_bf16=with_bf16)

    def trunk(x, hg_s0, rw_s0, shift0):
        B, T, _ = x.shape
        M = B * T
        x0 = x.reshape(M, D)
        tm_up = _row_tile(M, 1024)
        tm_dn = _row_tile(M, 512)
        x1, x1b = ffn(x0, x0.astype(BF16), f1_in, f1_dn, row(ln1_g), row(ln1_b), tm_dn, True)
        z = _zproj(x1b, w_in_b, n_cols=lay.width, tm=tm_up, tn=1024)
        zg = _zproj(x1b, w_gates_b, n_cols=2 * D, tm=tm_up, tn=1024)
        o, hg_s = _hgrn(z, lb, row(hg_norm_g), hg_s0, B=B, T=T, lay=lay)
        st = jnp.swapaxes(rw_s0, -1, -2).reshape(B, npair, 2, RW_HEAD_DIM, RW_HEAD_DIM)
        zero = jnp.zeros_like(st[:, :, 0])
        h0 = jnp.concatenate([jnp.concatenate([st[:, :, 0], zero], -1),
                              jnp.concatenate([zero, st[:, :, 1]], -1)], -2)
        shift_z = lay.rw_in_to_z(shift0)
        lw, a_lr, g = _rwkv_lora(z, mu_z, shift_z, prm, B=B, T=T, lay=lay)
        y, h_out = _rwkv(z, lw, a_lr, g, mu_z, shift_z, prm, h0, B=B, T=T, lay=lay)
        hd = RW_HEAD_DIM
        rw_s = jnp.stack([h_out[:, :, :hd, :hd], h_out[:, :, hd:, hd:]], axis=2)
        rw_s = jnp.swapaxes(rw_s.reshape(B, nrw, hd, hd), -1, -2)
        m = _merge_up(o, y, zg, wa_b, wb_b, tm=tm_up, tn=512, lay=lay)
        x2, x2b = _down_ln(m, wo_b, x1, row(ln2_g), row(ln2_b), alpha=alpha, scale=1.0,
                           tm=tm_dn, tn=min(OUT_PROJ_COLS, D), with_bf16=True)
        (x3,) = ffn(x2, x2b, f2_in, f2_dn, row(ln3_g), row(ln3_b), tm_dn, False)
        shift = lay.z_to_rw_in(z.reshape(B, T, -1)[:, -1:, :])
        return x3.reshape(B, T, D), hg_s[None], rw_s[None], shift[None]

    Bp = x_prompt.shape[0]
    hg0 = jnp.zeros((Bp, nh, HG_HEAD_DIM, HG_HEAD_DIM), F32)
    rw0 = jnp.zeros((Bp, nrw, RW_HEAD_DIM, RW_HEAD_DIM), F32)
    sh0 = jnp.zeros((Bp, 1, rw_mu.shape[-1]), F32)
    y_p, hg_p, rw_p, sh_p = trunk(x_prompt, hg0, rw0, sh0)
    y_s, hg_s, rw_s, sh_s = trunk(x_sample, state_hgrn[l].astype(F32), state_rwkv[l].astype(F32),
                                  state_shift[l].astype(F32))
    return (y_p, y_s, hg_p, rw_p, sh_p, hg_s, rw_s, sh_s)
```

```python
import functools
import math

import jax
import jax.numpy as jnp
from jax import lax
from jax.experimental import pallas as pl
from jax.experimental.pallas import tpu as pltpu

F32 = jnp.float32
BF16 = jnp.bfloat16

LANES = 128
HG_HEAD_DIM = 128
RW_HEAD_DIM = 64
RW_PAIR = 2 * RW_HEAD_DIM
HG_BLOCK = 16
RW_CHUNK = 64
RECURRENCE_ROWS = 512
HG_ROWS = 256
HG_BLOCKS_PER_STEP = 64
LOG2_E = 1.4426950408889634
RW_ROWS = 128
RW_CHUNKS_PER_STEP = 32
LN_EPS = 1e-5
RMS_EPS = 1e-6
RW_GN_EPS = 64e-5
VMEM_LIMIT = 56 * 1024 * 1024
DOWN_VMEM_LIMIT = 60 * 1024 * 1024
PANEL_ROWS = 32
FFN_DOWN_COLS = 256
OUT_PROJ_COLS = 1024

_NN = (((1,), (0,)), ((), ()))
_NT = (((1,), (1,)), ((), ()))
_TN = (((0,), (0,)), ((), ()))


def _mm(a, b, dims=_NN):
    return lax.dot_general(a.astype(BF16), b.astype(BF16), dims, preferred_element_type=F32)


def _split(x, n):
    parts = []
    for _ in range(n - 1):
        p = x.astype(BF16)
        parts.append(p)
        x = x - p.astype(F32)
    parts.append(x.astype(BF16))
    return parts


def _mm_wide(a, b):
    ah, am = _split(a, 2)
    bh, bm = _split(b, 2)
    dot = lambda p, q: jnp.dot(p, q, preferred_element_type=F32)
    return dot(ah, bh) + (dot(ah, bm) + dot(am, bh))


def _mask_mm(mask, x, terms=3):
    mb = mask.astype(BF16)
    return sum(jnp.dot(mb, p, preferred_element_type=F32) for p in _split(x, terms))


def _layer_norm(y, g, b):
    mu = jnp.mean(y, -1, keepdims=True)
    d = y - mu
    var = jnp.mean(d * d, -1, keepdims=True)
    return d * lax.rsqrt(var + LN_EPS) * g + b


def _params(*sem):
    return pltpu.CompilerParams(dimension_semantics=sem, vmem_limit_bytes=VMEM_LIMIT)


def _gated_up_body(x_ref, wg_ref, wu_ref, o_ref):
    x = x_ref[...]
    h = jnp.dot(x, wg_ref[...], preferred_element_type=F32)
    u = jnp.dot(x, wu_ref[...], preferred_element_type=F32)
    o_ref[...] = (h * jax.nn.sigmoid(h) * u).astype(o_ref.dtype)


def _gated_up(xb, w_in_b, *, tm, tn):
    M, D = xb.shape
    F = w_in_b.shape[1] // 2
    nj = F // tn
    return pl.pallas_call(
        _gated_up_body,
        grid=(M // tm, nj),
        in_specs=[pl.BlockSpec((tm, D), lambda i, j: (i, 0)),
                  pl.BlockSpec((D, tn), lambda i, j: (0, j)),
                  pl.BlockSpec((D, tn), lambda i, j: (0, j + nj))],
        out_specs=pl.BlockSpec((tm, tn), lambda i, j: (i, j)),
        out_shape=jax.ShapeDtypeStruct((M, F), BF16),
        compiler_params=_params("parallel", "arbitrary"),
        name="gated_up",
    )(xb, w_in_b, w_in_b)


def _down_ln_body(a_ref, w_ref, x_ref, g_ref, b_ref, o_ref, *maybe_ob_ref, alpha, scale, nj, tn):
    j = pl.program_id(1)
    c0 = pl.multiple_of(j * tn, tn)
    o_ref[:, pl.ds(c0, tn)] = alpha * x_ref[...] + scale * jnp.dot(a_ref[...], w_ref[...],
                                                                   preferred_element_type=F32)

    @pl.when(j == nj - 1)
    def _():
        g = g_ref[...]
        b = b_ref[...]
        nrow = min(PANEL_ROWS, o_ref.shape[0])

        def panel(i, carry):
            r0 = pl.multiple_of(i * nrow, nrow)
            rows = pl.ds(r0, nrow)
            y = _layer_norm(o_ref[rows, :], g, b)
            o_ref[rows, :] = y
            for ob_ref in maybe_ob_ref:
                ob_ref[rows, :] = y.astype(ob_ref.dtype)
            return carry

        lax.fori_loop(0, o_ref.shape[0] // nrow, panel, 0)


def _down_ln(act, w_b, x, ln_g, ln_b, *, alpha, scale, tm, tn, with_bf16):
    M, D = x.shape
    K = act.shape[1]
    nj = D // tn
    row_spec = pl.BlockSpec((tm, D), lambda i, j: (i, 0))
    out_shape = [jax.ShapeDtypeStruct((M, D), F32)] + ([jax.ShapeDtypeStruct((M, D), BF16)] if with_bf16 else [])
    return pl.pallas_call(
        functools.partial(_down_ln_body, alpha=alpha, scale=scale, nj=nj, tn=tn),
        grid=(M // tm, nj),
        in_specs=[pl.BlockSpec((tm, K), lambda i, j: (i, 0)),
                  pl.BlockSpec((K, tn), lambda i, j: (0, j)),
                  pl.BlockSpec((tm, tn), lambda i, j: (i, j)),
                  pl.BlockSpec((1, D), lambda i, j: (0, 0)),
                  pl.BlockSpec((1, D), lambda i, j: (0, 0))],
        out_specs=[row_spec] * len(out_shape),
        out_shape=out_shape,
        compiler_params=pltpu.CompilerParams(dimension_semantics=("parallel", "arbitrary"),
                                             vmem_limit_bytes=DOWN_VMEM_LIMIT),
        name="down_ln",
    )(act, w_b, x, ln_g, ln_b)


def _zproj_body(x_ref, w_ref, o_ref):
    o_ref[...] = jnp.dot(x_ref[...], w_ref[...], preferred_element_type=F32)


def _zproj(xb, w_b, *, n_cols, tm, tn):
    M, D = xb.shape
    N = n_cols
    assert N % tn == 0 and N <= w_b.shape[1]
    return pl.pallas_call(
        _zproj_body,
        grid=(M // tm, N // tn),
        in_specs=[pl.BlockSpec((tm, D), lambda i, j: (i, 0)), pl.BlockSpec((D, tn), lambda i, j: (0, j))],
        out_specs=pl.BlockSpec((tm, tn), lambda i, j: (i, j)),
        out_shape=jax.ShapeDtypeStruct((M, N), F32),
        compiler_params=_params("parallel", "arbitrary"),
        name="zproj",
    )(xb, w_b)


def _block_cumsum(x, c):
    n, w = x.shape
    nb = n // c
    ri = lax.broadcasted_iota(jnp.int32, (c, c), 0)
    ci = lax.broadcasted_iota(jnp.int32, (c, c), 1)
    wide = x if nb == 1 else jnp.concatenate([x[i * c:(i + 1) * c, :] for i in range(nb)], axis=1)
    out = _mask_mm(jnp.where(ci <= ri, 1.0, 0.0), wide)
    return out if nb == 1 else jnp.concatenate([out[:, i * w:(i + 1) * w] for i in range(nb)], axis=0)


def _hgrn_body(q_ref, f_ref, v_ref, gh_ref, lb_ref, ng_ref, s0_ref, o_ref, so_ref, st_ref, *, nseq, nhg, nt):
    t = pl.program_id(2)
    c = HG_BLOCK
    d = HG_HEAD_DIM
    rows_in = q_ref.shape[0]
    tb = rows_in * nhg
    nb = tb // c
    nseq_all = nseq * nhg
    to_rows = lambda x: x if nhg == 1 else jnp.concatenate([x[:, p * d:(p + 1) * d] for p in range(nhg)], axis=0)
    to_lanes = lambda x: x if nhg == 1 else jnp.concatenate([x[p * rows_in:(p + 1) * rows_in] for p in range(nhg)],
                                                            axis=1)

    @pl.when(t == 0)
    def _():
        for s in range(nseq):
            for p in range(nhg):
                st_ref[s, p] = s0_ref[s, p].T

    lb = lb_ref[...]
    q = to_rows(q_ref[...])
    fp = f_ref[...]
    v = to_rows(v_ref[...])
    logf = to_rows(jnp.log(lb + (1.0 - lb) * jax.nn.sigmoid(fp)))
    kf = to_rows((1.0 - lb) * jax.nn.sigmoid(-fp))
    L = _block_cumsum(logf, c)
    L3, q3, kf3, v3 = (x.reshape(nb, c, d) for x in (L, q, kf, v))
    l_end = L3[:, c - 1:c, :]
    qe = q * jnp.exp(L)
    ke = (kf3 * jnp.exp(l_end - L3)).reshape(tb, d)
    d_end = jnp.exp(l_end)

    h = c // 2
    rows = lax.broadcasted_iota(jnp.int32, (1, h, 1), 1)
    B3 = L3 * LOG2_E
    halves = [(B3[:, :h, :], q3[:, :h, :]), (B3[:, h:, :], q3[:, h:, :])]
    acc = [jnp.zeros((nb, h, d), F32), jnp.zeros((nb, h, d), F32)]
    for s in range(c):
        b_s, k_s, v_s = B3[:, s:s + 1, :], kf3[:, s:s + 1, :], v3[:, s:s + 1, :]
        for half, (b_t, q_t) in enumerate(halves):
            first = s - half * h
            if first >= h:
                continue
            diff = b_t - b_s
            if first > 0:
                diff = jnp.where(rows >= first, diff, -jnp.inf)
            sc = jnp.sum(jnp.exp2(diff) * q_t * k_s, -1, keepdims=True)
            acc[half] = acc[half] + sc * v_s
    o = jnp.concatenate(acc, axis=1).reshape(tb, d)

    blk = lambda x, i: x[i * c:(i + 1) * c, :]
    kv = [_mm(blk(v, i), blk(ke, i), _TN) for i in range(nb)]
    per_seq = nb // nseq_all
    states = [st_ref[s, p] for p in range(nhg) for s in range(nseq)]
    starts = []
    for i in range(nb):
        s = i // per_seq
        starts.append(states[s])
        states[s] = states[s] * d_end[i] + kv[i]
    for p in range(nhg):
        for s in range(nseq):
            st_ref[s, p] = states[p * nseq + s]
    o = o + jnp.concatenate([_mm(blk(qe, i), starts[i], _NT) for i in range(nb)], axis=0)

    o = to_lanes(o * lax.rsqrt(jnp.mean(o * o, -1, keepdims=True) + RMS_EPS))
    gh = gh_ref[...]
    o_ref[...] = (o * ng_ref[...] * (gh * jax.nn.sigmoid(gh))).astype(o_ref.dtype)

    @pl.when(t == nt - 1)
    def _():
        for p in range(nhg):
            for s in range(nseq):
                so_ref[s, p] = states[p * nseq + s].T


def _seq_tiling(B, T, rows):
    if T >= rows:
        tb = _row_tile(T, rows)
        return 1, tb, T // tb
    nseq = _row_tile(B, max(1, rows // T))
    return nseq, nseq * T, 1


def _hgrn(z, lb, norm_g, s0, *, B, T, lay):
    nh = lay.d_hg // HG_HEAD_DIM
    nseq, tb, nt = _seq_tiling(B, T, HG_ROWS)
    nhg = math.gcd(nh, max(1, HG_BLOCKS_PER_STEP // (tb // HG_BLOCK)))
    d = HG_HEAD_DIM
    w = nhg * d
    ng = nh // nhg
    zspec = lambda sec: pl.BlockSpec((tb, w), lambda b, h, t: (b * nt + t, sec * ng + h))
    pspec = pl.BlockSpec((1, w), lambda b, h, t: (0, h))
    sspec = pl.BlockSpec((nseq, nhg, d, d), lambda b, h, t: (b, h, 0, 0))
    return pl.pallas_call(
        functools.partial(_hgrn_body, nseq=nseq, nhg=nhg, nt=nt),
        grid=(B // nseq, ng, nt),
        in_specs=[zspec(0), zspec(1), zspec(2), zspec(3), pspec, pspec, sspec],
        out_specs=[pl.BlockSpec((tb, w), lambda b, h, t: (b * nt + t, h)), sspec],
        out_shape=[jax.ShapeDtypeStruct((B * T, lay.d_hg), BF16), jax.ShapeDtypeStruct(s0.shape, F32)],
        scratch_shapes=[pltpu.VMEM((nseq, nhg, d, d), F32)],
        compiler_params=_params("parallel", "parallel", "arbitrary"),
        name="hgrn",
    )(z, z, z, z, lb, norm_g, s0)


def _rwkv_chunks(r, lw, k, v, a, b, states, C):
    tb, W = r.shape
    nc = tb // C
    C2 = 2 * C
    chunks = range(nc)
    Lc = _block_cumsum(lw, C)
    Lc3 = Lc.reshape(nc, C, W)
    l_end = Lc3[:, C - 1:C, :]
    e_inv = jnp.exp(-Lc)
    e_end = jnp.exp(l_end - Lc3).reshape(tb, W)
    w_end = jnp.exp(l_end)
    in_a = lax.broadcasted_iota(jnp.int32, (1, W), 1) < (W // 2)

    def stacked(x):
        xa = jnp.where(in_a, x, 0.0)
        xb = x - xa
        return [jnp.concatenate([xa[c * C:(c + 1) * C], xb[c * C:(c + 1) * C]], axis=0) for c in chunks]

    Rs = stacked(r * jnp.exp(Lc))
    As = [x.astype(BF16) for x in stacked(a * jnp.exp(Lc - lw))]
    Bs = stacked(b * e_inv)
    Ks = stacked(k * e_inv)
    Bhs = [x.astype(BF16) for x in stacked(b * e_end)]
    Khs = [x.astype(BF16) for x in stacked(k * e_end)]
    Vs = [x.astype(BF16) for x in stacked(v)]

    row = lax.broadcasted_iota(jnp.int32, (C2, C2), 0)
    col = lax.broadcasted_iota(jnp.int32, (C2, C2), 1)
    strict = col < row
    incl = col <= row
    if C2 % LANES == 0:
        sc = [_mm(jnp.concatenate([As[c], Rs[c]], axis=0), jnp.concatenate([Bs[c], Ks[c]], axis=0), _NT)
              for c in chunks]
        s_ab = [x[:C2, :C2] for x in sc]
        s_ak = [x[:C2, C2:] for x in sc]
        s_rb = [x[C2:, :C2] for x in sc]
        s_rk = [x[C2:, C2:] for x in sc]
    else:
        s_ab = [_mm(As[c], Bs[c], _NT) for c in chunks]
        s_ak = [_mm(As[c], Ks[c], _NT) for c in chunks]
        s_rb = [_mm(Rs[c], Bs[c], _NT) for c in chunks]
        s_rk = [_mm(Rs[c], Ks[c], _NT) for c in chunks]
    a_ab = [jnp.where(strict, x, 0.0) for x in s_ab]
    a_ak = [jnp.where(strict, x, 0.0).astype(BF16) for x in s_ak]
    a_rb = [jnp.where(incl, x, 0.0).astype(BF16) for x in s_rb]
    a_rk = [jnp.where(incl, x, 0.0).astype(BF16) for x in s_rk]

    eye = jnp.where(row == col, 1.0, 0.0)
    tinv = [eye + x for x in a_ab]
    pw = [_mm(x, x) for x in a_ab]
    levels = int(math.log2(C)) - 1
    for lvl in range(levels):
        if lvl == levels - 1:
            tinv = [tinv[c] + _mm(pw[c], tinv[c]) for c in chunks]
        elif C2 % LANES == 0:
            both = [_mm(pw[c], jnp.concatenate([pw[c], tinv[c]], axis=1)) for c in chunks]
            pw = [x[:, :C2] for x in both]
            tinv = [tinv[c] + both[c][:, C2:] for c in chunks]
        else:
            tinv = [tinv[c] + _mm(pw[c], tinv[c]) for c in chunks]
            pw = [_mm(x, x) for x in pw]

    av = [_mm(a_ak[c], Vs[c]) for c in chunks]
    pq = [_mm(tinv[c], jnp.concatenate([As[c], av[c].astype(BF16)], axis=1)) for c in chunks]
    pq_b = [x.astype(BF16) for x in pq]
    bp = [_mm(Bhs[c], pq_b[c], _TN) for c in chunks]
    kv = [_mm(Khs[c], Vs[c], _TN) for c in chunks]
    rq = [_mm(a_rb[c], pq_b[c]) for c in chunks]
    ry = [Rs[c] + rq[c][:, :W] for c in chunks]
    yc = [rq[c][:, W:] + _mm(a_rk[c], Vs[c]) for c in chunks]

    rk = lax.broadcasted_iota(jnp.int32, (W, W), 0)
    ck = lax.broadcasted_iota(jnp.int32, (W, W), 1)
    diag = rk == ck
    states = list(states)
    per_seq = nc // len(states)
    starts = []
    for c in chunks:
        s = c // per_seq
        starts.append(states[s])
        M = jnp.where(diag, jnp.broadcast_to(w_end[c], (W, W)), 0.0) + bp[c][:, :W]
        states[s] = _mm(M, states[s]) + (bp[c][:, W:] + kv[c])
    ys = [_mm(ry[c], starts[c]) + yc[c] for c in chunks]
    ys = [x[:C] + x[C:] for x in ys]
    return (ys[0] if nc == 1 else jnp.concatenate(ys, axis=0)), states


def _token_shift(ref, carry, mu_ref, nseq):
    x = ref[...]
    tb, w = x.shape
    tseq = tb // nseq
    seq_start = lax.broadcasted_iota(jnp.int32, (tb, 1), 0) % tseq == 0
    before = jnp.broadcast_to(carry[...], (nseq, tseq, w)).reshape(tb, w)
    prev = jnp.where(seq_start, before, pltpu.roll(x, 1, 0))
    carry[...] = x.reshape(nseq, tseq, w)[:, tseq - 1:tseq, :]
    return x + (prev - x) * mu_ref[...]


def _rwkv_lora_body(lo_ref, mu_ref, s0_ref, w0_ref, w2_ref, a0_ref, a2_ref, g2_ref,
                    lw_ref, a_ref, g_ref, carry, *, nseq, dl_w, dl_a, dl_g):
    @pl.when(pl.program_id(1) == 0)
    def _():
        carry[...] = s0_ref[...]

    lo = _token_shift(lo_ref, carry, mu_ref, nseq)
    wd = lo[:, :dl_w]
    ad = lo[:, dl_w:dl_w + dl_a]
    gd = lo[:, dl_w + dl_a:dl_w + dl_a + dl_g]
    w_log = -jax.nn.softplus(-(w0_ref[...] + _mm_wide(jnp.tanh(wd), w2_ref[...]))) - 0.5
    lw_ref[...] = -jnp.exp(w_log)
    a_ref[...] = jax.nn.sigmoid(a0_ref[...] + _mm(ad, a2_ref[...]))
    g_ref[...] = _mm(jax.nn.sigmoid(gd), g2_ref[...]).astype(g_ref.dtype)


def _rwkv_lora(z, mu_z, shift_z, prm, *, B, T, lay):
    nseq, tb, nt = _seq_tiling(B, T, RECURRENCE_ROWS)
    lw_ = lay.lora_w
    c_lo = lay.off_lora // lw_
    d_rw = lay.d_rw
    dl_w, dl_a, dl_g = prm["w2"].shape[0], prm["a2"].shape[0], prm["g2"].shape[0]
    full = lambda a: pl.BlockSpec(a.shape, lambda b, t: (0, 0))
    out_spec = pl.BlockSpec((tb, d_rw), lambda b, t: (b * nt + t, 0))
    return pl.pallas_call(
        functools.partial(_rwkv_lora_body, nseq=nseq, dl_w=dl_w, dl_a=dl_a, dl_g=dl_g),
        grid=(B // nseq, nt),
        in_specs=[pl.BlockSpec((tb, lw_), lambda b, t: (b * nt + t, c_lo)),
                  pl.BlockSpec((1, lw_), lambda b, t: (0, c_lo)),
                  pl.BlockSpec((nseq, 1, lw_), lambda b, t: (b, 0, c_lo)),
                  full(prm["w0"]), full(prm["w2"]), full(prm["a0"]), full(prm["a2"]), full(prm["g2"])],
        out_specs=[out_spec, out_spec, out_spec],
        out_shape=[jax.ShapeDtypeStruct((B * T, d_rw), F32), jax.ShapeDtypeStruct((B * T, d_rw), F32),
                   jax.ShapeDtypeStruct((B * T, d_rw), BF16)],
        scratch_shapes=[pltpu.VMEM((nseq, 1, lw_), F32)],
        compiler_params=_params("parallel", "arbitrary"),
        name="rwkv_lora",
    )(z, mu_z, shift_z, prm["w0"], prm["w2"], prm["a0"], prm["a2"], prm["g2"])


def _rwkv_body(r_ref, k_ref, v_ref, lw_ref, a_ref, g_ref, mur_ref, muk_ref, muv_ref,
               sr_ref, sk_ref, sv_ref, kk_ref, ka_ref, rk_ref, lng_ref, lnb_ref, h0_ref,
               y_ref, ho_ref, h_sc, cr_sc, ck_sc, cv_sc, *, nseq, npg, chunk, nt):
    t = pl.program_id(2)
    w = RW_PAIR

    @pl.when(t == 0)
    def _():
        h_sc[...] = h0_ref[...]
        cr_sc[...] = sr_ref[...]
        ck_sc[...] = sk_ref[...]
        cv_sc[...] = sv_ref[...]

    rs = _token_shift(r_ref, cr_sc, mur_ref, nseq)
    ks = _token_shift(k_ref, ck_sc, muk_ref, nseq)
    vs = _token_shift(v_ref, cv_sc, muv_ref, nseq)
    a_lr = a_ref[...]
    tb = rs.shape[0]
    to_rows = lambda x: x if npg == 1 else jnp.concatenate([x[:, p * w:(p + 1) * w] for p in range(npg)], axis=0)
    to_lanes = lambda x: x if npg == 1 else jnp.concatenate([x[p * tb:(p + 1) * tb] for p in range(npg)], axis=1)

    in_a = lax.broadcasted_iota(jnp.int32, (1, w), 1) < RW_HEAD_DIM

    def head_sum(x):
        total = jnp.sum(x, -1, keepdims=True)
        first = jnp.sum(jnp.where(in_a, x, 0.0), -1, keepdims=True)
        return jnp.where(in_a, first, total - first)

    kk = to_rows(ks * kk_ref[...])
    kk = kk / jnp.maximum(jnp.sqrt(head_sum(kk * kk)), 1e-12)
    k2 = ks * (1.0 + (a_lr - 1.0) * ka_ref[...])
    a_rows = to_rows(a_lr)
    v_rows = to_rows(vs)

    states = [h_sc[s, p] for p in range(npg) for s in range(nseq)]
    y, states = _rwkv_chunks(to_rows(rs), to_rows(lw_ref[...]), to_rows(k2), v_rows, -kk, kk * a_rows, states, chunk)
    for p in range(npg):
        for s in range(nseq):
            h_sc[s, p] = states[p * nseq + s]

    inv_n = 1.0 / RW_HEAD_DIM
    mu = head_sum(y) * inv_n
    dev = y - mu
    var = head_sum(dev * dev) * inv_n
    yn = to_lanes(dev * lax.rsqrt(var + RW_GN_EPS)) * lng_ref[...] + lnb_ref[...]
    bonus = to_lanes(head_sum(to_rows(rs * k2 * rk_ref[...])) * v_rows)
    y_ref[...] = ((yn + bonus) * g_ref[...].astype(F32)).astype(y_ref.dtype)

    @pl.when(t == nt - 1)
    def _():
        ho_ref[...] = h_sc[...]


def _rwkv(z, lw, a_lr, g, mu_z, shift_z, prm, h0, *, B, T, lay):
    npair = lay.d_rw // RW_PAIR
    nseq, tb, nt = _seq_tiling(B, T, RW_ROWS)
    chunk = min(RW_CHUNK, tb // nseq)
    npg = math.gcd(npair, max(1, RW_CHUNKS_PER_STEP // (tb // chunk)))
    assert npair % npg == 0 and lay.off_r % (npg * RW_PAIR) == 0
    w = npg * RW_PAIR
    c_r, c_k, c_v = (lay.off_r // w, (lay.off_r + lay.d_rw) // w, (lay.off_r + 2 * lay.d_rw) // w)
    zs = lambda c0: pl.BlockSpec((tb, w), lambda b, p, t: (b * nt + t, c0 + p))
    ms = lambda c0: pl.BlockSpec((1, w), lambda b, p, t: (0, c0 + p))
    ss = lambda c0: pl.BlockSpec((nseq, 1, w), lambda b, p, t: (b, 0, c0 + p))
    ps = pl.BlockSpec((1, w), lambda b, p, t: (0, p))
    hs = pl.BlockSpec((nseq, npg, RW_PAIR, RW_PAIR), lambda b, p, t: (b, p, 0, 0))
    return pl.pallas_call(
        functools.partial(_rwkv_body, nseq=nseq, npg=npg, chunk=chunk, nt=nt),
        grid=(B // nseq, npair // npg, nt),
        in_specs=[
            zs(c_r), zs(c_k), zs(c_v), zs(0), zs(0), zs(0),
            ms(c_r), ms(c_k), ms(c_v),
            ss(c_r), ss(c_k), ss(c_v),
            ps, ps, ps, ps, ps, hs,
        ],
        out_specs=[pl.BlockSpec((tb, w), lambda b, p, t: (b * nt + t, p)), hs],
        out_shape=[jax.ShapeDtypeStruct((B * T, lay.d_rw), BF16), jax.ShapeDtypeStruct(h0.shape, F32)],
        scratch_shapes=[pltpu.VMEM((nseq, npg, RW_PAIR, RW_PAIR), F32), pltpu.VMEM((nseq, 1, w), F32),
                        pltpu.VMEM((nseq, 1, w), F32), pltpu.VMEM((nseq, 1, w), F32)],
        compiler_params=_params("parallel", "parallel", "arbitrary"),
        name="rwkv",
    )(z, z, z, lw, a_lr, g, mu_z, mu_z, mu_z, shift_z, shift_z, shift_z,
      prm["k_k"], prm["k_a"], prm["r_k"], prm["ln_g"], prm["ln_b"], h0)


def _merge_up_body(o_ref, y_ref, ga_ref, gb_ref, wa_ref, wb_ref, m_ref):
    ua = jnp.dot(o_ref[...], wa_ref[...], preferred_element_type=F32)
    ub = jnp.dot(y_ref[...], wb_ref[...], preferred_element_type=F32)
    m = jax.nn.sigmoid(ga_ref[...]) * ua + jax.nn.sigmoid(gb_ref[...]) * ub
    m_ref[...] = m.astype(m_ref.dtype)


def _merge_up(o, y, zg, wa_b, wb_b, *, tm, tn, lay):
    M = o.shape[0]
    D = lay.d_model
    c_ga, c_gb = 0, D // tn
    return pl.pallas_call(
        _merge_up_body,
        grid=(M // tm, D // tn),
        in_specs=[
            pl.BlockSpec((tm, lay.d_hg), lambda i, j: (i, 0)),
            pl.BlockSpec((tm, lay.d_rw), lambda i, j: (i, 0)),
            pl.BlockSpec((tm, tn), lambda i, j: (i, c_ga + j)),
            pl.BlockSpec((tm, tn), lambda i, j: (i, c_gb + j)),
            pl.BlockSpec((lay.d_hg, tn), lambda i, j: (0, j)),
            pl.BlockSpec((lay.d_rw, tn), lambda i, j: (0, j)),
        ],
        out_specs=pl.BlockSpec((tm, tn), lambda i, j: (i, j)),
        out_shape=jax.ShapeDtypeStruct((M, D), BF16),
        compiler_params=_params("parallel", "arbitrary"),
        name="merge_up",
    )(o, y, zg, zg, wa_b, wb_b)


class _ZLayout:
    def __init__(self, d_model, dl_w, dl_a, dl_g):
        self.d_hg = d_model // 2
        self.d_rw = d_model // 2
        self.d_model = d_model
        self.n_lora = dl_w + dl_a + dl_g
        self.off_r = 4 * self.d_hg
        self.off_lora = self.off_r + 3 * self.d_rw
        self.off_gates = self.off_lora + self.n_lora
        self.lora_w = 1024
        assert self.n_lora <= self.lora_w and self.off_lora % self.lora_w == 0
        self.width = self.off_lora + self.lora_w
        self.n_rw_in = 3 * self.d_rw + self.n_lora

    def rw_in_to_z(self, a):
        pads = [(0, 0)] * (a.ndim - 1) + [(self.off_r, self.width - self.off_r - self.n_rw_in)]
        return jnp.pad(a, pads)

    def z_to_rw_in(self, zrow):
        return zrow[..., self.off_r:self.off_r + self.n_rw_in]


def _row_tile(m, pref):
    t = min(pref, m)
    while m % t:
        t //= 2
    return t


def kernel(x_prompt, x_sample, state_hgrn, state_rwkv, state_shift, ln1_g, ln1_b, ffn1_w_in, ffn1_w_down,
           ln2_g, ln2_b, w_in, hg_lb, hg_norm_g, hg_proj, rw_mu, rw_w0, rw_w2, rw_a0, rw_a2, rw_g2, rw_k_k,
           rw_k_a, rw_r_k, rw_ln_g, rw_ln_b, rw_proj, w_out, ln3_g, ln3_b, ffn2_w_in, ffn2_w_down):
    depth = ffn1_w_in.shape[0]
    assert depth == 1, "single-layer stack"
    D = x_prompt.shape[-1]
    alpha = (2 * depth) ** 0.25
    dl_w, dl_a, dl_g = rw_w2.shape[1], rw_a2.shape[1], rw_g2.shape[1]
    assert dl_w % LANES == 0 and dl_a % LANES == 0
    dl_gp = -(-dl_g // LANES) * LANES
    lay = _ZLayout(D, dl_w, dl_a, dl_g)
    nh = lay.d_hg // HG_HEAD_DIM
    nrw = lay.d_rw // RW_HEAD_DIM
    npair = nrw // 2
    l = 0

    row = lambda p: p[l].reshape(1, -1).astype(F32)
    f1_in, f1_dn = ffn1_w_in[l].astype(BF16), ffn1_w_down[l].astype(BF16)
    f2_in, f2_dn = ffn2_w_in[l].astype(BF16), ffn2_w_down[l].astype(BF16)
    w_in_b = w_in[l].astype(BF16)
    w_gates_b = w_in_b[:, lay.off_gates:]
    assert w_gates_b.shape[1] == 2 * D
    wa_b, wb_b, wo_b = hg_proj[l].astype(BF16), rw_proj[l].astype(BF16), w_out[l].astype(BF16)
    lb = jnp.cumsum(jax.nn.softmax(hg_lb.astype(F32), axis=0), axis=0)[l].reshape(1, -1)
    mu_z = lay.rw_in_to_z(rw_mu[l].reshape(1, -1))
    prm = dict(w0=row(rw_w0), w2=rw_w2[l], a0=row(rw_a0), a2=rw_a2[l],
               g2=jnp.pad(rw_g2[l], ((0, dl_gp - dl_g), (0, 0))).astype(BF16),
               k_k=row(rw_k_k), k_a=row(rw_k_a), r_k=row(rw_r_k), ln_g=row(rw_ln_g), ln_b=row(rw_ln_b))

    def ffn(x, xb, w_in_b, w_dn_b, g, b, tm_dn, with_bf16):
        ff = w_dn_b.shape[0]
        tn_up = max(t for t in (LANES, 2 * LANES, 4 * LANES) if ff % t == 0)
        act = _gated_up(xb, w_in_b, tm=_row_tile(x.shape[0], 512 * 1024 // tn_up), tn=tn_up)
        return _down_ln(act, w_dn_b, x, g, b, alpha=alpha, scale=0.5, tm=tm_dn, tn=FFN_DOWN_COLS,
                        with_bf16=with_bf16)

    def trunk(x, hg_s0, rw_s0, shift0):
        B, T, _ = x.shape
        M = B * T
        x0 = x.reshape(M, D)
        tm_up = _row_tile(M, 1024)
        tm_dn = _row_tile(M, 512)
        x1, x1b = ffn(x0, x0.astype(BF16), f1_in, f1_dn, row(ln1_g), row(ln1_b), tm_dn, True)
        z = _zproj(x1b, w_in_b, n_cols=lay.width, tm=tm_up, tn=1024)
        zg = _zproj(x1b, w_gates_b, n_cols=2 * D, tm=tm_up, tn=1024)
        o, hg_s = _hgrn(z, lb, row(hg_norm_g), hg_s0, B=B, T=T, lay=lay)
        st = jnp.swapaxes(rw_s0, -1, -2).reshape(B, npair, 2, RW_HEAD_DIM, RW_HEAD_DIM)
        zero = jnp.zeros_like(st[:, :, 0])
        h0 = jnp.concatenate([jnp.concatenate([st[:, :, 0], zero], -1),
                              jnp.concatenate([zero, st[:, :, 1]], -1)], -2)
        shift_z = lay.rw_in_to_z(shift0)
        lw, a_lr, g = _rwkv_lora(z, mu_z, shift_z, prm, B=B, T=T, lay=lay)
        y, h_out = _rwkv(z, lw, a_lr, g, mu_z, shift_z, prm, h0, B=B, T=T, lay=lay)
        hd = RW_HEAD_DIM
        rw_s = jnp.stack([h_out[:, :, :hd, :hd], h_out[:, :, hd:, hd:]], axis=2)
        rw_s = jnp.swapaxes(rw_s.reshape(B, nrw, hd, hd), -1, -2)
        m = _merge_up(o, y, zg, wa_b, wb_b, tm=tm_up, tn=512, lay=lay)
        x2, x2b = _down_ln(m, wo_b, x1, row(ln2_g), row(ln2_b), alpha=alpha, scale=1.0,
                           tm=tm_dn, tn=min(OUT_PROJ_COLS, D), with_bf16=True)
        (x3,) = ffn(x2, x2b, f2_in, f2_dn, row(ln3_g), row(ln3_b), tm_dn, False)
        shift = lay.z_to_rw_in(z.reshape(B, T, -1)[:, -1:, :])
        return x3.reshape(B, T, D), hg_s[None], rw_s[None], shift[None]

    Bp = x_prompt.shape[0]
    hg0 = jnp.zeros((Bp, nh, HG_HEAD_DIM, HG_HEAD_DIM), F32)
    rw0 = jnp.zeros((Bp, nrw, RW_HEAD_DIM, RW_HEAD_DIM), F32)
    sh0 = jnp.zeros((Bp, 1, rw_mu.shape[-1]), F32)
    y_p, hg_p, rw_p, sh_p = trunk(x_prompt, hg0, rw0, sh0)
    y_s, hg_s, rw_s, sh_s = trunk(x_sample, state_hgrn[l].astype(F32), state_rwkv[l].astype(F32),
                                  state_shift[l].astype(F32))
    return (y_p, y_s, hg_p, rw_p, sh_p, hg_s, rw_s, sh_s)
```

```python
import functools
import math

import jax
import jax.numpy as jnp
from jax import lax
from jax.experimental import pallas as pl
from jax.experimental.pallas import tpu as pltpu

F32 = jnp.float32
BF16 = jnp.bfloat16

LANES = 128
HG_HEAD_DIM = 128
RW_HEAD_DIM = 64
RW_PAIR = 2 * RW_HEAD_DIM
HG_BLOCK = 16
RW_CHUNK = 64
RECURRENCE_ROWS = 512
HG_ROWS = 256
HG_BLOCKS_PER_STEP = 64
LOG2_E = 1.4426950408889634
ZG_PIECES = 4
RW_ROWS = 128
RW_CHUNKS_PER_STEP = 32
LN_EPS = 1e-5
RMS_EPS = 1e-6
RW_GN_EPS = 64e-5
VMEM_LIMIT = 56 * 1024 * 1024
DOWN_VMEM_LIMIT = 60 * 1024 * 1024
PANEL_ROWS = 32
FFN_DOWN_COLS = 256
OUT_PROJ_COLS = 1024

_NN = (((1,), (0,)), ((), ()))
_NT = (((1,), (1,)), ((), ()))
_TN = (((0,), (0,)), ((), ()))


def _mm(a, b, dims=_NN):
    return lax.dot_general(a.astype(BF16), b.astype(BF16), dims, preferred_element_type=F32)


def _split(x, n):
    parts = []
    for _ in range(n - 1):
        p = x.astype(BF16)
        parts.append(p)
        x = x - p.astype(F32)
    parts.append(x.astype(BF16))
    return parts


def _mm_wide(a, b):
    ah, am = _split(a, 2)
    bh, bm = _split(b, 2)
    dot = lambda p, q: jnp.dot(p, q, preferred_element_type=F32)
    return dot(ah, bh) + (dot(ah, bm) + dot(am, bh))


def _mask_mm(mask, x, terms=3):
    mb = mask.astype(BF16)
    return sum(jnp.dot(mb, p, preferred_element_type=F32) for p in _split(x, terms))


def _layer_norm(y, g, b):
    mu = jnp.mean(y, -1, keepdims=True)
    d = y - mu
    var = jnp.mean(d * d, -1, keepdims=True)
    return d * lax.rsqrt(var + LN_EPS) * g + b


def _params(*sem):
    return pltpu.CompilerParams(dimension_semantics=sem, vmem_limit_bytes=VMEM_LIMIT)


def _gated_up_body(x_ref, wg_ref, wu_ref, o_ref):
    x = x_ref[...]
    h = jnp.dot(x, wg_ref[...], preferred_element_type=F32)
    u = jnp.dot(x, wu_ref[...], preferred_element_type=F32)
    o_ref[...] = (h * jax.nn.sigmoid(h) * u).astype(o_ref.dtype)


def _gated_up(xb, w_in_b, *, tm, tn):
    M, D = xb.shape
    F = w_in_b.shape[1] // 2
    nj = F // tn
    return pl.pallas_call(
        _gated_up_body,
        grid=(M // tm, nj),
        in_specs=[pl.BlockSpec((tm, D), lambda i, j: (i, 0)),
                  pl.BlockSpec((D, tn), lambda i, j: (0, j)),
                  pl.BlockSpec((D, tn), lambda i, j: (0, j + nj))],
        out_specs=pl.BlockSpec((tm, tn), lambda i, j: (i, j)),
        out_shape=jax.ShapeDtypeStruct((M, F), BF16),
        compiler_params=_params("parallel", "arbitrary"),
        name="gated_up",
    )(xb, w_in_b, w_in_b)


def _down_ln_body(a_ref, w_ref, x_ref, g_ref, b_ref, o_ref, *maybe_ob_ref, alpha, scale, nj, tn):
    j = pl.program_id(1)
    c0 = pl.multiple_of(j * tn, tn)
    o_ref[:, pl.ds(c0, tn)] = alpha * x_ref[...] + scale * jnp.dot(a_ref[...], w_ref[...],
                                                                   preferred_element_type=F32)

    @pl.when(j == nj - 1)
    def _():
        g = g_ref[...]
        b = b_ref[...]
        nrow = min(PANEL_ROWS, o_ref.shape[0])

        def panel(i, carry):
            r0 = pl.multiple_of(i * nrow, nrow)
            rows = pl.ds(r0, nrow)
            y = _layer_norm(o_ref[rows, :], g, b)
            o_ref[rows, :] = y
            for ob_ref in maybe_ob_ref:
                ob_ref[rows, :] = y.astype(ob_ref.dtype)
            return carry

        lax.fori_loop(0, o_ref.shape[0] // nrow, panel, 0)


def _down_ln(act, w_b, x, ln_g, ln_b, *, alpha, scale, tm, tn, with_bf16):
    M, D = x.shape
    K = act.shape[1]
    nj = D // tn
    row_spec = pl.BlockSpec((tm, D), lambda i, j: (i, 0))
    out_shape = [jax.ShapeDtypeStruct((M, D), F32)] + ([jax.ShapeDtypeStruct((M, D), BF16)] if with_bf16 else [])
    return pl.pallas_call(
        functools.partial(_down_ln_body, alpha=alpha, scale=scale, nj=nj, tn=tn),
        grid=(M // tm, nj),
        in_specs=[pl.BlockSpec((tm, K), lambda i, j: (i, 0)),
                  pl.BlockSpec((K, tn), lambda i, j: (0, j)),
                  pl.BlockSpec((tm, tn), lambda i, j: (i, j)),
                  pl.BlockSpec((1, D), lambda i, j: (0, 0)),
                  pl.BlockSpec((1, D), lambda i, j: (0, 0))],
        out_specs=[row_spec] * len(out_shape),
        out_shape=out_shape,
        compiler_params=pltpu.CompilerParams(dimension_semantics=("parallel", "arbitrary"),
                                             vmem_limit_bytes=DOWN_VMEM_LIMIT),
        name="down_ln",
    )(act, w_b, x, ln_g, ln_b)


def _zproj_body(x_ref, w_ref, o_ref):
    o_ref[...] = jnp.dot(x_ref[...], w_ref[...], preferred_element_type=F32)


def _zproj(xb, w_b, *, n_cols, tm, tn):
    M, D = xb.shape
    N = n_cols
    assert N % tn == 0 and N <= w_b.shape[1]
    return pl.pallas_call(
        _zproj_body,
        grid=(M // tm, N // tn),
        in_specs=[pl.BlockSpec((tm, D), lambda i, j: (i, 0)), pl.BlockSpec((D, tn), lambda i, j: (0, j))],
        out_specs=pl.BlockSpec((tm, tn), lambda i, j: (i, j)),
        out_shape=jax.ShapeDtypeStruct((M, N), F32),
        compiler_params=_params("parallel", "arbitrary"),
        name="zproj",
    )(xb, w_b)


def _block_cumsum(x, c):
    n, w = x.shape
    nb = n // c
    ri = lax.broadcasted_iota(jnp.int32, (c, c), 0)
    ci = lax.broadcasted_iota(jnp.int32, (c, c), 1)
    wide = x if nb == 1 else jnp.concatenate([x[i * c:(i + 1) * c, :] for i in range(nb)], axis=1)
    out = _mask_mm(jnp.where(ci <= ri, 1.0, 0.0), wide)
    return out if nb == 1 else jnp.concatenate([out[:, i * w:(i + 1) * w] for i in range(nb)], axis=0)


def _block_cumsum_scan(x, c):
    pos = lax.broadcasted_iota(jnp.int32, (x.shape[0], 1), 0) % c
    shift = 1
    while shift < c:
        x = x + jnp.where(pos >= shift, pltpu.roll(x, shift, 0), 0.0)
        shift *= 2
    return x


def _hgrn_body(q_ref, f_ref, v_ref, gh_ref, lb_ref, ng_ref, s0_ref, o_ref, so_ref, st_ref, *, nseq, nhg, nt):
    _hgrn_step(q_ref, f_ref, v_ref, gh_ref, lb_ref, ng_ref, s0_ref, o_ref, so_ref, st_ref,
               t=pl.program_id(2), nseq=nseq, nhg=nhg, nt=nt)


def _hgrn_zg_body(x_ref, w_ref, q_ref, f_ref, v_ref, gh_ref, lb_ref, ng_ref, s0_ref,
                  zg_ref, o_ref, so_ref, st_ref, *, nhg, nt):
    kd = x_ref.shape[1] // ZG_PIECES

    def piece(i):
        def run():
            part = jnp.dot(x_ref[:, i * kd:(i + 1) * kd], w_ref[i * kd:(i + 1) * kd, :], preferred_element_type=F32)
            if i == 0:
                zg_ref[...] = part
            else:
                zg_ref[...] += part
        return run

    _hgrn_step(q_ref, f_ref, v_ref, gh_ref, lb_ref, ng_ref, s0_ref, o_ref, so_ref, st_ref,
               t=pl.program_id(0) % nt, nseq=1, nhg=nhg, nt=nt, side_work=[piece(i) for i in range(ZG_PIECES)])


def _hgrn_step(q_ref, f_ref, v_ref, gh_ref, lb_ref, ng_ref, s0_ref, o_ref, so_ref, st_ref, *, t, nseq, nhg, nt,
               side_work=()):
    c = HG_BLOCK
    d = HG_HEAD_DIM
    rows_in = q_ref.shape[0]
    tb = rows_in * nhg
    nb = tb // c
    nseq_all = nseq * nhg
    to_rows = lambda x: x if nhg == 1 else jnp.concatenate([x[:, p * d:(p + 1) * d] for p in range(nhg)], axis=0)
    to_lanes = lambda x: x if nhg == 1 else jnp.concatenate([x[p * rows_in:(p + 1) * rows_in] for p in range(nhg)],
                                                            axis=1)

    @pl.when(t == 0)
    def _():
        for s in range(nseq):
            for p in range(nhg):
                st_ref[s, p] = s0_ref[s, p].T

    lb = lb_ref[...]
    q = to_rows(q_ref[...])
    fp = f_ref[...]
    v = to_rows(v_ref[...])
    logf = to_rows(jnp.log(lb + (1.0 - lb) * jax.nn.sigmoid(fp)))
    kf = to_rows((1.0 - lb) * jax.nn.sigmoid(-fp))
    L = _block_cumsum_scan(logf, c)
    L3, q3, kf3, v3 = (x.reshape(nb, c, d) for x in (L, q, kf, v))
    l_end = L3[:, c - 1:c, :]
    qe = q * jnp.exp(L)
    ke = (kf3 * jnp.exp(l_end - L3)).reshape(tb, d)
    d_end = jnp.exp(l_end)

    h = c // 2
    rows = lax.broadcasted_iota(jnp.int32, (1, h, 1), 1)
    B3 = L3 * LOG2_E
    halves = [(B3[:, :h, :], q3[:, :h, :]), (B3[:, h:, :], q3[:, h:, :])]
    acc = [jnp.zeros((nb, h, d), F32), jnp.zeros((nb, h, d), F32)]
    every = c // len(side_work) if side_work else 0
    for s in range(c):
        if every and s % every == 0:
            side_work[s // every]()
        b_s, k_s, v_s = B3[:, s:s + 1, :], kf3[:, s:s + 1, :], v3[:, s:s + 1, :]
        for half, (b_t, q_t) in enumerate(halves):
            first = s - half * h
            if first >= h:
                continue
            diff = b_t - b_s
            if first > 0:
                diff = jnp.where(rows >= first, diff, -jnp.inf)
            sc = jnp.sum(jnp.exp2(diff) * q_t * k_s, -1, keepdims=True)
            acc[half] = acc[half] + sc * v_s
    o = jnp.concatenate(acc, axis=1).reshape(tb, d)

    blk = lambda x, i: x[i * c:(i + 1) * c, :]
    kv = [_mm(blk(v, i), blk(ke, i), _TN) for i in range(nb)]
    per_seq = nb // nseq_all
    states = [st_ref[s, p] for p in range(nhg) for s in range(nseq)]
    starts = []
    for i in range(nb):
        s = i // per_seq
        starts.append(states[s])
        states[s] = states[s] * d_end[i] + kv[i]
    for p in range(nhg):
        for s in range(nseq):
            st_ref[s, p] = states[p * nseq + s]
    o = o + jnp.concatenate([_mm(blk(qe, i), starts[i], _NT) for i in range(nb)], axis=0)

    o = to_lanes(o * lax.rsqrt(jnp.mean(o * o, -1, keepdims=True) + RMS_EPS))
    gh = gh_ref[...]
    o_ref[...] = (o * ng_ref[...] * (gh * jax.nn.sigmoid(gh))).astype(o_ref.dtype)

    @pl.when(t == nt - 1)
    def _():
        for p in range(nhg):
            for s in range(nseq):
                so_ref[s, p] = states[p * nseq + s].T


def _seq_tiling(B, T, rows):
    if T >= rows:
        tb = _row_tile(T, rows)
        return 1, tb, T // tb
    nseq = _row_tile(B, max(1, rows // T))
    return nseq, nseq * T, 1


def _hgrn_zg_tiles(B, T, lay):
    nh = lay.d_hg // HG_HEAD_DIM
    nseq, tb, nt = _seq_tiling(B, T, HG_ROWS)
    nhg = math.gcd(nh, max(1, HG_BLOCKS_PER_STEP // (tb // HG_BLOCK)))
    steps = (nh // nhg) * nt
    tm = _row_tile(B * T, 1024)
    n_i = B * T // tm
    if B != 1 or steps % n_i:
        return None
    n_j = steps // n_i
    n_cols = 2 * lay.d_model
    if n_cols % n_j or (n_cols // n_j) % LANES:
        return None
    return dict(tb=tb, nt=nt, nhg=nhg, tm=tm, n_j=n_j, tn=n_cols // n_j)


def _hgrn_zg(z, xb, w_gates_b, lb, norm_g, s0, *, tiles, lay):
    tb, nt, nhg, tm, n_j, tn = (tiles[k] for k in ("tb", "nt", "nhg", "tm", "n_j", "tn"))
    nh = lay.d_hg // HG_HEAD_DIM
    d = HG_HEAD_DIM
    w = nhg * d
    ng = nh // nhg
    M, D = xb.shape
    zspec = lambda sec: pl.BlockSpec((tb, w), lambda s: (s % nt, sec * ng + s // nt))
    pspec = pl.BlockSpec((1, w), lambda s: (0, s // nt))
    sspec = pl.BlockSpec((1, nhg, d, d), lambda s: (0, s // nt, 0, 0))
    return pl.pallas_call(
        functools.partial(_hgrn_zg_body, nhg=nhg, nt=nt),
        grid=(ng * nt,),
        in_specs=[pl.BlockSpec((tm, D), lambda s: (s // n_j, 0)),
                  pl.BlockSpec((D, tn), lambda s: (0, s % n_j)),
                  zspec(0), zspec(1), zspec(2), zspec(3), pspec, pspec, sspec],
        out_specs=[pl.BlockSpec((tm, tn), lambda s: (s // n_j, s % n_j)),
                   pl.BlockSpec((tb, w), lambda s: (s % nt, s // nt)), sspec],
        out_shape=[jax.ShapeDtypeStruct((M, n_j * tn), F32), jax.ShapeDtypeStruct((M, lay.d_hg), BF16),
                   jax.ShapeDtypeStruct(s0.shape, F32)],
        scratch_shapes=[pltpu.VMEM((1, nhg, d, d), F32)],
        compiler_params=_params("arbitrary"),
        name="hgrn_zg",
    )(xb, w_gates_b, z, z, z, z, lb, norm_g, s0)


def _hgrn(z, lb, norm_g, s0, *, B, T, lay):
    nh = lay.d_hg // HG_HEAD_DIM
    nseq, tb, nt = _seq_tiling(B, T, HG_ROWS)
    nhg = math.gcd(nh, max(1, HG_BLOCKS_PER_STEP // (tb // HG_BLOCK)))
    d = HG_HEAD_DIM
    w = nhg * d
    ng = nh // nhg
    zspec = lambda sec: pl.BlockSpec((tb, w), lambda b, h, t: (b * nt + t, sec * ng + h))
    pspec = pl.BlockSpec((1, w), lambda b, h, t: (0, h))
    sspec = pl.BlockSpec((nseq, nhg, d, d), lambda b, h, t: (b, h, 0, 0))
    return pl.pallas_call(
        functools.partial(_hgrn_body, nseq=nseq, nhg=nhg, nt=nt),
        grid=(B // nseq, ng, nt),
        in_specs=[zspec(0), zspec(1), zspec(2), zspec(3), pspec, pspec, sspec],
        out_specs=[pl.BlockSpec((tb, w), lambda b, h, t: (b * nt + t, h)), sspec],
        out_shape=[jax.ShapeDtypeStruct((B * T, lay.d_hg), BF16), jax.ShapeDtypeStruct(s0.shape, F32)],
        scratch_shapes=[pltpu.VMEM((nseq, nhg, d, d), F32)],
        compiler_params=_params("parallel", "parallel", "arbitrary"),
        name="hgrn",
    )(z, z, z, z, lb, norm_g, s0)


def _rwkv_chunks(r, lw, k, v, a, b, states, C):
    tb, W = r.shape
    nc = tb // C
    C2 = 2 * C
    chunks = range(nc)
    Lc = _block_cumsum(lw, C)
    Lc3 = Lc.reshape(nc, C, W)
    l_end = Lc3[:, C - 1:C, :]
    e_inv = jnp.exp(-Lc)
    e_end = jnp.exp(l_end - Lc3).reshape(tb, W)
    w_end = jnp.exp(l_end)
    in_a = lax.broadcasted_iota(jnp.int32, (1, W), 1) < (W // 2)

    def stacked(x):
        xa = jnp.where(in_a, x, 0.0)
        xb = x - xa
        return [jnp.concatenate([xa[c * C:(c + 1) * C], xb[c * C:(c + 1) * C]], axis=0) for c in chunks]

    Rs = stacked(r * jnp.exp(Lc))
    As = [x.astype(BF16) for x in stacked(a * jnp.exp(Lc - lw))]
    Bs = stacked(b * e_inv)
    Ks = stacked(k * e_inv)
    Bhs = [x.astype(BF16) for x in stacked(b * e_end)]
    Khs = [x.astype(BF16) for x in stacked(k * e_end)]
    Vs = [x.astype(BF16) for x in stacked(v)]

    row = lax.broadcasted_iota(jnp.int32, (C2, C2), 0)
    col = lax.broadcasted_iota(jnp.int32, (C2, C2), 1)
    strict = col < row
    incl = col <= row
    if C2 % LANES == 0:
        sc = [_mm(jnp.concatenate([As[c], Rs[c]], axis=0), jnp.concatenate([Bs[c], Ks[c]], axis=0), _NT)
              for c in chunks]
        s_ab = [x[:C2, :C2] for x in sc]
        s_ak = [x[:C2, C2:] for x in sc]
        s_rb = [x[C2:, :C2] for x in sc]
        s_rk = [x[C2:, C2:] for x in sc]
    else:
        s_ab = [_mm(As[c], Bs[c], _NT) for c in chunks]
        s_ak = [_mm(As[c], Ks[c], _NT) for c in chunks]
        s_rb = [_mm(Rs[c], Bs[c], _NT) for c in chunks]
        s_rk = [_mm(Rs[c], Ks[c], _NT) for c in chunks]
    a_ab = [jnp.where(strict, x, 0.0) for x in s_ab]
    a_ak = [jnp.where(strict, x, 0.0).astype(BF16) for x in s_ak]
    a_rb = [jnp.where(incl, x, 0.0).astype(BF16) for x in s_rb]
    a_rk = [jnp.where(incl, x, 0.0).astype(BF16) for x in s_rk]

    eye = jnp.where(row == col, 1.0, 0.0)
    tinv = [eye + x for x in a_ab]
    pw = [_mm(x, x) for x in a_ab]
    levels = int(math.log2(C)) - 1
    for lvl in range(levels):
        if lvl == levels - 1:
            tinv = [tinv[c] + _mm(pw[c], tinv[c]) for c in chunks]
        elif C2 % LANES == 0:
            both = [_mm(pw[c], jnp.concatenate([pw[c], tinv[c]], axis=1)) for c in chunks]
            pw = [x[:, :C2] for x in both]
            tinv = [tinv[c] + both[c][:, C2:] for c in chunks]
        else:
            tinv = [tinv[c] + _mm(pw[c], tinv[c]) for c in chunks]
            pw = [_mm(x, x) for x in pw]

    av = [_mm(a_ak[c], Vs[c]) for c in chunks]
    pq = [_mm(tinv[c], jnp.concatenate([As[c], av[c].astype(BF16)], axis=1)) for c in chunks]
    pq_b = [x.astype(BF16) for x in pq]
    bp = [_mm(Bhs[c], pq_b[c], _TN) for c in chunks]
    kv = [_mm(Khs[c], Vs[c], _TN) for c in chunks]
    rq = [_mm(a_rb[c], pq_b[c]) for c in chunks]
    ry = [Rs[c] + rq[c][:, :W] for c in chunks]
    yc = [rq[c][:, W:] + _mm(a_rk[c], Vs[c]) for c in chunks]

    rk = lax.broadcasted_iota(jnp.int32, (W, W), 0)
    ck = lax.broadcasted_iota(jnp.int32, (W, W), 1)
    diag = rk == ck
    states = list(states)
    per_seq = nc // len(states)
    starts = []
    for c in chunks:
        s = c // per_seq
        starts.append(states[s])
        M = jnp.where(diag, jnp.broadcast_to(w_end[c], (W, W)), 0.0) + bp[c][:, :W]
        states[s] = _mm(M, states[s]) + (bp[c][:, W:] + kv[c])
    ys = [_mm(ry[c], starts[c]) + yc[c] for c in chunks]
    ys = [x[:C] + x[C:] for x in ys]
    return (ys[0] if nc == 1 else jnp.concatenate(ys, axis=0)), states


def _token_shift(ref, carry, mu_ref, nseq):
    x = ref[...]
    tb, w = x.shape
    tseq = tb // nseq
    seq_start = lax.broadcasted_iota(jnp.int32, (tb, 1), 0) % tseq == 0
    before = jnp.broadcast_to(carry[...], (nseq, tseq, w)).reshape(tb, w)
    prev = jnp.where(seq_start, before, pltpu.roll(x, 1, 0))
    carry[...] = x.reshape(nseq, tseq, w)[:, tseq - 1:tseq, :]
    return x + (prev - x) * mu_ref[...]


def _rwkv_lora_body(lo_ref, mu_ref, s0_ref, w0_ref, w2_ref, a0_ref, a2_ref, g2_ref,
                    lw_ref, a_ref, g_ref, carry, *, nseq, dl_w, dl_a, dl_g):
    @pl.when(pl.program_id(1) == 0)
    def _():
        carry[...] = s0_ref[...]

    lo = _token_shift(lo_ref, carry, mu_ref, nseq)
    wd = lo[:, :dl_w]
    ad = lo[:, dl_w:dl_w + dl_a]
    gd = lo[:, dl_w + dl_a:dl_w + dl_a + dl_g]
    u = w0_ref[...] + _mm_wide(jnp.tanh(wd), w2_ref[...])
    lw_ref[...] = -math.exp(-0.5) * jax.nn.sigmoid(u)
    a_ref[...] = jax.nn.sigmoid(a0_ref[...] + _mm(ad, a2_ref[...]))
    g_ref[...] = _mm(jax.nn.sigmoid(gd), g2_ref[...]).astype(g_ref.dtype)


def _rwkv_lora(z, mu_z, shift_z, prm, *, B, T, lay):
    nseq, tb, nt = _seq_tiling(B, T, RECURRENCE_ROWS)
    lw_ = lay.lora_w
    c_lo = lay.off_lora // lw_
    d_rw = lay.d_rw
    dl_w, dl_a, dl_g = prm["w2"].shape[0], prm["a2"].shape[0], prm["g2"].shape[0]
    full = lambda a: pl.BlockSpec(a.shape, lambda b, t: (0, 0))
    out_spec = pl.BlockSpec((tb, d_rw), lambda b, t: (b * nt + t, 0))
    return pl.pallas_call(
        functools.partial(_rwkv_lora_body, nseq=nseq, dl_w=dl_w, dl_a=dl_a, dl_g=dl_g),
        grid=(B // nseq, nt),
        in_specs=[pl.BlockSpec((tb, lw_), lambda b, t: (b * nt + t, c_lo)),
                  pl.BlockSpec((1, lw_), lambda b, t: (0, c_lo)),
                  pl.BlockSpec((nseq, 1, lw_), lambda b, t: (b, 0, c_lo)),
                  full(prm["w0"]), full(prm["w2"]), full(prm["a0"]), full(prm["a2"]), full(prm["g2"])],
        out_specs=[out_spec, out_spec, out_spec],
        out_shape=[jax.ShapeDtypeStruct((B * T, d_rw), F32), jax.ShapeDtypeStruct((B * T, d_rw), F32),
                   jax.ShapeDtypeStruct((B * T, d_rw), BF16)],
        scratch_shapes=[pltpu.VMEM((nseq, 1, lw_), F32)],
        compiler_params=_params("parallel", "arbitrary"),
        name="rwkv_lora",
    )(z, mu_z, shift_z, prm["w0"], prm["w2"], prm["a0"], prm["a2"], prm["g2"])


def _rwkv_body(r_ref, k_ref, v_ref, lw_ref, a_ref, g_ref, mur_ref, muk_ref, muv_ref,
               sr_ref, sk_ref, sv_ref, kk_ref, ka_ref, rk_ref, lng_ref, lnb_ref, h0_ref,
               y_ref, ho_ref, h_sc, cr_sc, ck_sc, cv_sc, *, nseq, npg, chunk, nt):
    t = pl.program_id(2)
    w = RW_PAIR

    @pl.when(t == 0)
    def _():
        h_sc[...] = h0_ref[...]
        cr_sc[...] = sr_ref[...]
        ck_sc[...] = sk_ref[...]
        cv_sc[...] = sv_ref[...]

    rs = _token_shift(r_ref, cr_sc, mur_ref, nseq)
    ks = _token_shift(k_ref, ck_sc, muk_ref, nseq)
    vs = _token_shift(v_ref, cv_sc, muv_ref, nseq)
    a_lr = a_ref[...]
    tb = rs.shape[0]
    to_rows = lambda x: x if npg == 1 else jnp.concatenate([x[:, p * w:(p + 1) * w] for p in range(npg)], axis=0)
    to_lanes = lambda x: x if npg == 1 else jnp.concatenate([x[p * tb:(p + 1) * tb] for p in range(npg)], axis=1)

    in_a = lax.broadcasted_iota(jnp.int32, (1, w), 1) < RW_HEAD_DIM

    def head_sum(x):
        total = jnp.sum(x, -1, keepdims=True)
        first = jnp.sum(jnp.where(in_a, x, 0.0), -1, keepdims=True)
        return jnp.where(in_a, first, total - first)

    kk = to_rows(ks * kk_ref[...])
    kk = kk / jnp.maximum(jnp.sqrt(head_sum(kk * kk)), 1e-12)
    k2 = ks * (1.0 + (a_lr - 1.0) * ka_ref[...])
    a_rows = to_rows(a_lr)
    v_rows = to_rows(vs)

    states = [h_sc[s, p] for p in range(npg) for s in range(nseq)]
    y, states = _rwkv_chunks(to_rows(rs), to_rows(lw_ref[...]), to_rows(k2), v_rows, -kk, kk * a_rows, states, chunk)
    for p in range(npg):
        for s in range(nseq):
            h_sc[s, p] = states[p * nseq + s]

    inv_n = 1.0 / RW_HEAD_DIM
    mu = head_sum(y) * inv_n
    dev = y - mu
    var = head_sum(dev * dev) * inv_n
    yn = to_lanes(dev * lax.rsqrt(var + RW_GN_EPS)) * lng_ref[...] + lnb_ref[...]
    bonus = to_lanes(head_sum(to_rows(rs * k2 * rk_ref[...])) * v_rows)
    y_ref[...] = ((yn + bonus) * g_ref[...].astype(F32)).astype(y_ref.dtype)

    @pl.when(t == nt - 1)
    def _():
        ho_ref[...] = h_sc[...]


def _rwkv(z, lw, a_lr, g, mu_z, shift_z, prm, h0, *, B, T, lay):
    npair = lay.d_rw // RW_PAIR
    nseq, tb, nt = _seq_tiling(B, T, RW_ROWS)
    chunk = min(RW_CHUNK, tb // nseq)
    npg = math.gcd(npair, max(1, RW_CHUNKS_PER_STEP // (tb // chunk)))
    assert npair % npg == 0 and lay.off_r % (npg * RW_PAIR) == 0
    w = npg * RW_PAIR
    c_r, c_k, c_v = (lay.off_r // w, (lay.off_r + lay.d_rw) // w, (lay.off_r + 2 * lay.d_rw) // w)
    zs = lambda c0: pl.BlockSpec((tb, w), lambda b, p, t: (b * nt + t, c0 + p))
    ms = lambda c0: pl.BlockSpec((1, w), lambda b, p, t: (0, c0 + p))
    ss = lambda c0: pl.BlockSpec((nseq, 1, w), lambda b, p, t: (b, 0, c0 + p))
    ps = pl.BlockSpec((1, w), lambda b, p, t: (0, p))
    hs = pl.BlockSpec((nseq, npg, RW_PAIR, RW_PAIR), lambda b, p, t: (b, p, 0, 0))
    return pl.pallas_call(
        functools.partial(_rwkv_body, nseq=nseq, npg=npg, chunk=chunk, nt=nt),
        grid=(B // nseq, npair // npg, nt),
        in_specs=[
            zs(c_r), zs(c_k), zs(c_v), zs(0), zs(0), zs(0),
            ms(c_r), ms(c_k), ms(c_v),
            ss(c_r), ss(c_k), ss(c_v),
            ps, ps, ps, ps, ps, hs,
        ],
        out_specs=[pl.BlockSpec((tb, w), lambda b, p, t: (b * nt + t, p)), hs],
        out_shape=[jax.ShapeDtypeStruct((B * T, lay.d_rw), BF16), jax.ShapeDtypeStruct(h0.shape, F32)],
        scratch_shapes=[pltpu.VMEM((nseq, npg, RW_PAIR, RW_PAIR), F32), pltpu.VMEM((nseq, 1, w), F32),
                        pltpu.VMEM((nseq, 1, w), F32), pltpu.VMEM((nseq, 1, w), F32)],
        compiler_params=_params("parallel", "parallel", "arbitrary"),
        name="rwkv",
    )(z, z, z, lw, a_lr, g, mu_z, mu_z, mu_z, shift_z, shift_z, shift_z,
      prm["k_k"], prm["k_a"], prm["r_k"], prm["ln_g"], prm["ln_b"], h0)


def _merge_up_body(o_ref, y_ref, ga_ref, gb_ref, wa_ref, wb_ref, m_ref):
    ua = jnp.dot(o_ref[...], wa_ref[...], preferred_element_type=F32)
    ub = jnp.dot(y_ref[...], wb_ref[...], preferred_element_type=F32)
    m = jax.nn.sigmoid(ga_ref[...]) * ua + jax.nn.sigmoid(gb_ref[...]) * ub
    m_ref[...] = m.astype(m_ref.dtype)


def _merge_up(o, y, zg, wa_b, wb_b, *, tm, tn, lay):
    M = o.shape[0]
    D = lay.d_model
    c_ga, c_gb = 0, D // tn
    return pl.pallas_call(
        _merge_up_body,
        grid=(M // tm, D // tn),
        in_specs=[
            pl.BlockSpec((tm, lay.d_hg), lambda i, j: (i, 0)),
            pl.BlockSpec((tm, lay.d_rw), lambda i, j: (i, 0)),
            pl.BlockSpec((tm, tn), lambda i, j: (i, c_ga + j)),
            pl.BlockSpec((tm, tn), lambda i, j: (i, c_gb + j)),
            pl.BlockSpec((lay.d_hg, tn), lambda i, j: (0, j)),
            pl.BlockSpec((lay.d_rw, tn), lambda i, j: (0, j)),
        ],
        out_specs=pl.BlockSpec((tm, tn), lambda i, j: (i, j)),
        out_shape=jax.ShapeDtypeStruct((M, D), BF16),
        compiler_params=_params("parallel", "arbitrary"),
        name="merge_up",
    )(o, y, zg, zg, wa_b, wb_b)


class _ZLayout:
    def __init__(self, d_model, dl_w, dl_a, dl_g):
        self.d_hg = d_model // 2
        self.d_rw = d_model // 2
        self.d_model = d_model
        self.n_lora = dl_w + dl_a + dl_g
        self.off_r = 4 * self.d_hg
        self.off_lora = self.off_r + 3 * self.d_rw
        self.off_gates = self.off_lora + self.n_lora
        self.lora_w = 1024
        assert self.n_lora <= self.lora_w and self.off_lora % self.lora_w == 0
        self.width = self.off_lora + self.lora_w
        self.n_rw_in = 3 * self.d_rw + self.n_lora

    def rw_in_to_z(self, a):
        pads = [(0, 0)] * (a.ndim - 1) + [(self.off_r, self.width - self.off_r - self.n_rw_in)]
        return jnp.pad(a, pads)

    def z_to_rw_in(self, zrow):
        return zrow[..., self.off_r:self.off_r + self.n_rw_in]


def _row_tile(m, pref):
    t = min(pref, m)
    while m % t:
        t //= 2
    return t


def kernel(x_prompt, x_sample, state_hgrn, state_rwkv, state_shift, ln1_g, ln1_b, ffn1_w_in, ffn1_w_down,
           ln2_g, ln2_b, w_in, hg_lb, hg_norm_g, hg_proj, rw_mu, rw_w0, rw_w2, rw_a0, rw_a2, rw_g2, rw_k_k,
           rw_k_a, rw_r_k, rw_ln_g, rw_ln_b, rw_proj, w_out, ln3_g, ln3_b, ffn2_w_in, ffn2_w_down):
    depth = ffn1_w_in.shape[0]
    assert depth == 1, "single-layer stack"
    D = x_prompt.shape[-1]
    alpha = (2 * depth) ** 0.25
    dl_w, dl_a, dl_g = rw_w2.shape[1], rw_a2.shape[1], rw_g2.shape[1]
    assert dl_w % LANES == 0 and dl_a % LANES == 0
    dl_gp = -(-dl_g // LANES) * LANES
    lay = _ZLayout(D, dl_w, dl_a, dl_g)
    nh = lay.d_hg // HG_HEAD_DIM
    nrw = lay.d_rw // RW_HEAD_DIM
    npair = nrw // 2
    l = 0

    row = lambda p: p[l].reshape(1, -1).astype(F32)
    f1_in, f1_dn = ffn1_w_in[l].astype(BF16), ffn1_w_down[l].astype(BF16)
    f2_in, f2_dn = ffn2_w_in[l].astype(BF16), ffn2_w_down[l].astype(BF16)
    w_in_b = w_in[l].astype(BF16)
    w_gates_b = w_in_b[:, lay.off_gates:]
    assert w_gates_b.shape[1] == 2 * D
    wa_b, wb_b, wo_b = hg_proj[l].astype(BF16), rw_proj[l].astype(BF16), w_out[l].astype(BF16)
    lb = jnp.cumsum(jax.nn.softmax(hg_lb.astype(F32), axis=0), axis=0)[l].reshape(1, -1)
    mu_z = lay.rw_in_to_z(rw_mu[l].reshape(1, -1))
    prm = dict(w0=row(rw_w0), w2=rw_w2[l], a0=row(rw_a0), a2=rw_a2[l],
               g2=jnp.pad(rw_g2[l], ((0, dl_gp - dl_g), (0, 0))).astype(BF16),
               k_k=row(rw_k_k), k_a=row(rw_k_a), r_k=row(rw_r_k), ln_g=row(rw_ln_g), ln_b=row(rw_ln_b))

    def ffn(x, xb, w_in_b, w_dn_b, g, b, tm_dn, with_bf16):
        ff = w_dn_b.shape[0]
        tn_up = max(t for t in (LANES, 2 * LANES, 4 * LANES) if ff % t == 0)
        act = _gated_up(xb, w_in_b, tm=_row_tile(x.shape[0], 512 * 1024 // tn_up), tn=tn_up)
        return _down_ln(act, w_dn_b, x, g, b, alpha=alpha, scale=0.5, tm=tm_dn, tn=FFN_DOWN_COLS,
                        with_bf16=with_bf16)

    def trunk(x, hg_s0, rw_s0, shift0):
        B, T, _ = x.shape
        M = B * T
        x0 = x.reshape(M, D)
        tm_up = _row_tile(M, 1024)
        tm_dn = _row_tile(M, 512)
        x1, x1b = ffn(x0, x0.astype(BF16), f1_in, f1_dn, row(ln1_g), row(ln1_b), tm_dn, True)
        z = _zproj(x1b, w_in_b, n_cols=lay.width, tm=tm_up, tn=1024)
        tiles = _hgrn_zg_tiles(B, T, lay)
        if tiles is None:
            zg = _zproj(x1b, w_gates_b, n_cols=2 * D, tm=tm_up, tn=1024)
            o, hg_s = _hgrn(z, lb, row(hg_norm_g), hg_s0, B=B, T=T, lay=lay)
        else:
            zg, o, hg_s = _hgrn_zg(z, x1b, w_gates_b, lb, row(hg_norm_g), hg_s0, tiles=tiles, lay=lay)
        st = jnp.swapaxes(rw_s0, -1, -2).reshape(B, npair, 2, RW_HEAD_DIM, RW_HEAD_DIM)
        zero = jnp.zeros_like(st[:, :, 0])
        h0 = jnp.concatenate([jnp.concatenate([st[:, :, 0], zero], -1),
                              jnp.concatenate([zero, st[:, :, 1]], -1)], -2)
        shift_z = lay.rw_in_to_z(shift0)
        lw, a_lr, g = _rwkv_lora(z, mu_z, shift_z, prm, B=B, T=T, lay=lay)
        y, h_out = _rwkv(z, lw, a_lr, g, mu_z, shift_z, prm, h0, B=B, T=T, lay=lay)
        hd = RW_HEAD_DIM
        rw_s = jnp.stack([h_out[:, :, :hd, :hd], h_out[:, :, hd:, hd:]], axis=2)
        rw_s = jnp.swapaxes(rw_s.reshape(B, nrw, hd, hd), -1, -2)
        m = _merge_up(o, y, zg, wa_b, wb_b, tm=tm_up, tn=512, lay=lay)
        x2, x2b = _down_ln(m, wo_b, x1, row(ln2_g), row(ln2_b), alpha=alpha, scale=1.0,
                           tm=tm_dn, tn=min(OUT_PROJ_COLS, D), with_bf16=True)
        (x3,) = ffn(x2, x2b, f2_in, f2_dn, row(ln3_g), row(ln3_b), tm_dn, False)
        shift = lay.z_to_rw_in(z.reshape(B, T, -1)[:, -1:, :])
        return x3.reshape(B, T, D), hg_s[None], rw_s[None], shift[None]

    Bp = x_prompt.shape[0]
    hg0 = jnp.zeros((Bp, nh, HG_HEAD_DIM, HG_HEAD_DIM), F32)
    rw0 = jnp.zeros((Bp, nrw, RW_HEAD_DIM, RW_HEAD_DIM), F32)
    sh0 = jnp.zeros((Bp, 1, rw_mu.shape[-1]), F32)
    y_p, hg_p, rw_p, sh_p = trunk(x_prompt, hg0, rw0, sh0)
    y_s, hg_s, rw_s, sh_s = trunk(x_sample, state_hgrn[l].astype(F32), state_rwkv[l].astype(F32),
                                  state_shift[l].astype(F32))
    return (y_p, y_s, hg_p, rw_p, sh_p, hg_s, rw_s, sh_s)
```
